```python
import jax
import jax.numpy as jnp
from jax import lax
import numpy as np

D_MODEL = 1024
BATCH = 8
SEQ = 2048
DEPTH = 4
DEC_BATCH = 32
DEC_SEQ = 8
PAST_LEN = 16384
PAGE_SIZE = 128

N_EVEN = (DEPTH + 1) // 2
N_ODD = DEPTH // 2
HEAD_DIM = 64
Q_BLOCK = 128
RMS_EPS = 1e-6
MLA_HEADS = 8
MLA_Q_RANK = 256
MLA_KV_RANK = 128
MLA_NOPE = 64
MLA_ROPE = 32
MLA_V = 64
ROPE_THETA = 10000.0
MLA_SCALE = (MLA_NOPE + MLA_ROPE) ** -0.5
MOBA_HEADS = 8
MOBA_BLOCK = 256
MOBA_TOPK = 3
MOBA_SCALE = HEAD_DIM ** -0.5
PAGES_PER_BLOCK = MOBA_BLOCK // PAGE_SIZE
FOX_HEADS = 16
FOX_KV_HEADS = 8
FOX_GROUP = FOX_HEADS // FOX_KV_HEADS
FOX_SCALE = HEAD_DIM ** -0.5
FORGET_BIAS = 2.0
FFN_HIDDEN = ((8 * D_MODEL // 3 + 255) // 256) * 256
EVEN_IN = MLA_Q_RANK + MLA_KV_RANK + MLA_ROPE + 3 * MOBA_HEADS * HEAD_DIM
EVEN_MIX = MLA_HEADS * MLA_V + MOBA_HEADS * HEAD_DIM
ODD_IN = (FOX_HEADS + 2 * FOX_KV_HEADS) * HEAD_DIM + FOX_HEADS
ODD_MIX = FOX_HEADS * HEAD_DIM

kernel_name = 'hybrid_mla_moba_fox_decode_step'


def rmsnorm(x, g):
    x32 = x.astype(jnp.float32)
    y = x32 * lax.rsqrt(jnp.mean(x32 * x32, axis=-1, keepdims=True) + RMS_EPS)
    return (y * g.astype(jnp.float32)).astype(x.dtype)


def rope(x, pos):
    half = x.shape[-1] // 2
    inv = ROPE_THETA ** (-jnp.arange(half, dtype=jnp.float32) / half)
    ang = pos.astype(jnp.float32)[:, None] * inv[None, :]
    cos, sin = jnp.cos(ang), jnp.sin(ang)
    x32 = x.astype(jnp.float32)
    x1, x2 = x32[..., :half], x32[..., half:]
    return jnp.concatenate([x1 * cos - x2 * sin, x1 * sin + x2 * cos], axis=-1).astype(x.dtype)


def alibi_slopes(n):
    return jnp.exp2(-8.0 * jnp.arange(1, n + 1, dtype=jnp.float32) / n)


def masked_softmax(logits, mask):
    return jax.nn.softmax(jnp.where(mask, logits, -jnp.inf), axis=-1)


def swiglu_block(x, g, w_gate, w_up, w_down):
    h = rmsnorm(x, g)
    return x + (jax.nn.silu(h @ w_gate) * (h @ w_up)) @ w_down


def mla_project(z, pos, q_norm, kv_norm, w_uq, w_uk):
    c_q = rmsnorm(z[..., :MLA_Q_RANK], q_norm)
    ckv = rmsnorm(z[..., MLA_Q_RANK:MLA_Q_RANK + MLA_KV_RANK], kv_norm)
    kpe = rope(z[..., MLA_Q_RANK + MLA_KV_RANK:MLA_Q_RANK + MLA_KV_RANK + MLA_ROPE], pos)
    q = jnp.einsum('btr,rhd->bhtd', c_q, w_uq)
    q_lat = jnp.einsum('bhtd,rhd->bhtr', q[..., :MLA_NOPE], w_uk)
    q_pe = rope(q[..., MLA_NOPE:], pos)
    return q_lat, q_pe, ckv, kpe


def mla_attend(q_lat, q_pe, ckv, kpe, q_pos, k_pos):
    logits = (jnp.einsum('bhtr,bsr->bhts', q_lat, ckv)
              + jnp.einsum('bhtp,bsp->bhts', q_pe, kpe)).astype(jnp.float32) * MLA_SCALE
    p = masked_softmax(logits, k_pos[None, :] <= q_pos[:, None]).astype(ckv.dtype)
    return jnp.einsum('bhts,bsr->bhtr', p, ckv)


def mla_prompt(q_lat, q_pe, ckv, kpe):
    B, H, S, R = q_lat.shape
    nqb = S // Q_BLOCK
    k_pos = jnp.arange(S, dtype=jnp.int32)

    def block(args):
        ql, qp, qpos = args
        return mla_attend(ql, qp, ckv, kpe, qpos, k_pos)

    ql = q_lat.reshape(B, H, nqb, Q_BLOCK, R).transpose(2, 0, 1, 3, 4)
    qp = q_pe.reshape(B, H, nqb, Q_BLOCK, MLA_ROPE).transpose(2, 0, 1, 3, 4)
    o = lax.map(block, (ql, qp, k_pos.reshape(nqb, Q_BLOCK)))
    return o.transpose(1, 2, 0, 3, 4).reshape(B, H, S, R)


def mla_sample(q_lat, q_pe, ckv_new, kpe_new, pool_ckv, pool_kpe, e, page_table):
    T = q_lat.shape[2]
    past = page_table.shape[1] * PAGE_SIZE
    q_pos = past + jnp.arange(T, dtype=jnp.int32)
    k_pos = jnp.arange(past + T, dtype=jnp.int32)

    def one(args):
        pt, ql, qp, cn, kn = args
        ckv = jnp.concatenate([pool_ckv[e, pt].reshape(past, MLA_KV_RANK), cn], axis=0)
        kpe = jnp.concatenate([pool_kpe[e, pt].reshape(past, MLA_ROPE), kn], axis=0)
        return mla_attend(ql[None], qp[None], ckv[None], kpe[None], q_pos, k_pos)[0]

    return lax.map(one, (page_table, q_lat, q_pe, ckv_new, kpe_new))


def moba_select(q, block_mean, n_sel):
    s = jnp.einsum('bhtd,bhnd->bhtn', q.astype(jnp.float32), block_mean.astype(jnp.float32))
    return lax.top_k(s, n_sel)[1]


def moba_attend(q, q_pos, own_k, own_v, own_pos, sel_k, sel_v, sel_pos):
    slope = alibi_slopes(MOBA_HEADS)[None, :, None, None]
    own = (jnp.einsum('bhtd,bhsd->bhts', q, own_k).astype(jnp.float32) * MOBA_SCALE
           - slope * (q_pos[:, None] - own_pos[None, :]).astype(jnp.float32))
    own = jnp.where(own_pos[None, :] <= q_pos[:, None], own, -jnp.inf)
    if sel_k is None:
        p = jax.nn.softmax(own, axis=-1).astype(own_v.dtype)
        return jnp.einsum('bhts,bhsd->bhtd', p, own_v)
    sel = (jnp.einsum('bhtd,bhtsd->bhts', q, sel_k).astype(jnp.float32) * MOBA_SCALE
           - slope * (q_pos[:, None] - sel_pos).astype(jnp.float32))
    n = sel.shape[-1]
    p = jax.nn.softmax(jnp.concatenate([sel, own], axis=-1), axis=-1).astype(own_v.dtype)
    return (jnp.einsum('bhts,bhtsd->bhtd', p[..., :n], sel_v)
            + jnp.einsum('bhts,bhsd->bhtd', p[..., n:], own_v))


def moba_prompt(q, k, v):
    B, H, S, Dh = q.shape
    nbf = S // MOBA_BLOCK
    kb = k[:, :, :nbf * MOBA_BLOCK].reshape(B, H, nbf, MOBA_BLOCK, Dh)
    vb = v[:, :, :nbf * MOBA_BLOCK].reshape(B, H, nbf, MOBA_BLOCK, Dh)
    k_mean = kb.astype(jnp.float32).mean(axis=3)
    bi = jnp.arange(B)[:, None, None, None]
    hi = jnp.arange(H)[None, :, None, None]
    outs = []
    for q0 in range(0, S, Q_BLOCK):
        q1 = q0 + Q_BLOCK
        c = q0 // MOBA_BLOCK
        n_sel = min(MOBA_TOPK, c)
        qb = q[:, :, q0:q1]
        q_pos = jnp.arange(q0, q1, dtype=jnp.int32)
        own_pos = jnp.arange(c * MOBA_BLOCK, q1, dtype=jnp.int32)
        sel_k = sel_v = sel_pos = None
        if n_sel > 0:
            idx = moba_select(qb, k_mean[:, :, :c], n_sel)
            sel_k = kb[bi, hi, idx].reshape(B, H, Q_BLOCK, n_sel * MOBA_BLOCK, Dh)
            sel_v = vb[bi, hi, idx].reshape(B, H, Q_BLOCK, n_sel * MOBA_BLOCK, Dh)
            sel_pos = (idx[..., None] * MOBA_BLOCK + jnp.arange(MOBA_BLOCK)).reshape(B, H, Q_BLOCK, n_sel * MOBA_BLOCK)
        outs.append(moba_attend(qb, q_pos, k[:, :, c * MOBA_BLOCK:q1], v[:, :, c * MOBA_BLOCK:q1],
                                own_pos, sel_k, sel_v, sel_pos))
    return jnp.concatenate(outs, axis=2)


def moba_sample(q, k_new, v_new, pool_k, pool_v, e, page_table):
    B, H, T, Dh = q.shape
    n_pages = page_table.shape[1]
    past = n_pages * PAGE_SIZE
    c = past // MOBA_BLOCK
    n_sel = min(MOBA_TOPK, c)
    n_full = c * PAGES_PER_BLOCK
    q_pos = past + jnp.arange(T, dtype=jnp.int32)
    own_pos = jnp.arange(c * MOBA_BLOCK, past + T, dtype=jnp.int32)
    hi = jnp.arange(H)[:, None, None, None, None]
    rows = jnp.arange(PAGE_SIZE)

    def one(args):
        pt, qs, kn, vn = args
        own_k, own_v = kn, vn
        if n_full < n_pages:
            pk = pool_k[e, pt[n_full:]].reshape(-1, H, Dh).transpose(1, 0, 2)
            pv = pool_v[e, pt[n_full:]].reshape(-1, H, Dh).transpose(1, 0, 2)
            own_k = jnp.concatenate([pk, kn], axis=1)
            own_v = jnp.concatenate([pv, vn], axis=1)
        sel_k = sel_v = sel_pos = None
        if n_sel > 0:
            kp = pool_k[e, pt[:n_full]].astype(jnp.float32).reshape(c, MOBA_BLOCK, H, Dh)
            block_mean = kp.mean(axis=1).transpose(1, 0, 2)
            idx = moba_select(qs[None], block_mean[None], n_sel)[0]
            lp = idx[..., None] * PAGES_PER_BLOCK + jnp.arange(PAGES_PER_BLOCK)
            phys = pt[lp][..., None]
            sel_k = pool_k[e, phys, rows, hi].reshape(1, H, T, n_sel * MOBA_BLOCK, Dh)
            sel_v = pool_v[e, phys, rows, hi].reshape(1, H, T, n_sel * MOBA_BLOCK, Dh)
            sel_pos = (lp[..., None] * PAGE_SIZE + rows).reshape(1, H, T, n_sel * MOBA_BLOCK)
        return moba_attend(qs[None], q_pos, own_k[None], own_v[None], own_pos, sel_k, sel_v, sel_pos)[0]

    return lax.map(one, (page_table, q, k_new, v_new))


def even_project(h, pos, w_in, q_norm, kv_norm, w_uq, w_uk):
    z = h @ w_in
    B, T, _ = z.shape
    q_lat, q_pe, ckv, kpe = mla_project(z, pos, q_norm, kv_norm, w_uq, w_uk)
    off = MLA_Q_RANK + MLA_KV_RANK + MLA_ROPE
    w = MOBA_HEADS * HEAD_DIM
    mq, mk, mv = [z[..., off + i * w:off + (i + 1) * w].reshape(B, T, MOBA_HEADS, HEAD_DIM).transpose(0, 2, 1, 3)
                  for i in range(3)]
    return q_lat, q_pe, ckv, kpe, mq, mk, mv


def even_output(o_lat, o_moba, w_uv, w_out):
    B, H, T, _ = o_lat.shape
    mla = jnp.einsum('bhtr,rhv->bthv', o_lat, w_uv).reshape(B, T, MLA_HEADS * MLA_V)
    moba = o_moba.transpose(0, 2, 1, 3).reshape(B, T, MOBA_HEADS * HEAD_DIM)
    return jnp.concatenate([mla, moba], axis=-1) @ w_out


def fox_project(z, b_f):
    B, T, _ = z.shape
    nq = FOX_HEADS * HEAD_DIM
    nk = FOX_KV_HEADS * HEAD_DIM
    q = z[..., :nq].reshape(B, T, FOX_KV_HEADS, FOX_GROUP, HEAD_DIM).transpose(0, 2, 3, 1, 4)
    k = z[..., nq:nq + nk].reshape(B, T, FOX_KV_HEADS, HEAD_DIM).transpose(0, 2, 1, 3)
    v = z[..., nq + nk:nq + 2 * nk].reshape(B, T, FOX_KV_HEADS, HEAD_DIM).transpose(0, 2, 1, 3)
    logf = jax.nn.log_sigmoid(z[..., nq + 2 * nk:].astype(jnp.float32) + b_f.astype(jnp.float32))
    return q, k, v, logf


def fox_attend(q, k, v, bias, mask):
    logits = jnp.einsum('bhgtd,bhsd->bhgts', q, k).astype(jnp.float32) * FOX_SCALE + bias
    p = masked_softmax(logits, mask).astype(v.dtype)
    return jnp.einsum('bhgts,bhsd->bhgtd', p, v)


def fox_prompt(q, k, v, logf):
    B, Hk, G, S, D = q.shape
    F = jnp.cumsum(logf, axis=1).reshape(B, S, Hk, G).transpose(0, 2, 3, 1)
    nqb = S // Q_BLOCK
    k_pos = jnp.arange(S, dtype=jnp.int32)

    def block(args):
        qb, Fq, qpos = args
        bias = Fq[..., None] - F[..., None, :]
        return fox_attend(qb, k, v, bias, k_pos[None, :] <= qpos[:, None])

    qs = q.reshape(B, Hk, G, nqb, Q_BLOCK, D).transpose(3, 0, 1, 2, 4, 5)
    Fs = F.reshape(B, Hk, G, nqb, Q_BLOCK).transpose(3, 0, 1, 2, 4)
    o = lax.map(block, (qs, Fs, k_pos.reshape(nqb, Q_BLOCK)))
    return o.transpose(1, 2, 3, 0, 4, 5).reshape(B, Hk, G, S, D)


def fox_sample(q, k_new, v_new, logf_new, pool_k, pool_v, pool_logf, o_idx, page_table):
    B, Hk, G, T, D = q.shape
    past = page_table.shape[1] * PAGE_SIZE
    q_pos = past + jnp.arange(T, dtype=jnp.int32)
    k_pos = jnp.arange(past + T, dtype=jnp.int32)
    mask = k_pos[None, :] <= q_pos[:, None]

    def one(args):
        pt, qs, kn, vn, lf = args
        kp = pool_k[o_idx, pt].reshape(past, Hk, D).transpose(1, 0, 2)
        vp = pool_v[o_idx, pt].reshape(past, Hk, D).transpose(1, 0, 2)
        lp = pool_logf[o_idx, pt].reshape(past, FOX_HEADS).astype(jnp.float32)
        rev = lax.cumsum(lp, axis=0, reverse=True)
        suffix = jnp.concatenate([rev[1:], jnp.zeros_like(rev[:1])], axis=0)
        gc = jnp.cumsum(lf, axis=0)
        bias = jnp.concatenate([suffix[None] + gc[:, None], gc[:, None] - gc[None]], axis=1)
        bias = bias.reshape(T, past + T, Hk, G).transpose(2, 3, 0, 1)
        kk = jnp.concatenate([kp, kn], axis=1)
        vv = jnp.concatenate([vp, vn], axis=1)
        return fox_attend(qs[None], kk[None], vv[None], bias[None], mask)[0]

    return lax.map(one, (page_table, q, k_new, v_new, logf_new))


def fox_output(o, w_out):
    B, Hk, G, T, D = o.shape
    return o.transpose(0, 3, 1, 2, 4).reshape(B, T, Hk * G * D) @ w_out


def setup_inputs(seed: int = 0) -> dict:
    key = jax.random.key(seed)
    ks = iter(jax.random.split(key, 40))
    n_pages = PAST_LEN // PAGE_SIZE
    n_phys = (DEC_BATCH * n_pages * 5) // 4

    def nrm(shape, scale=1.0):
        return jax.random.normal(next(ks), shape, jnp.float32) * scale

    def gain(shape):
        return 1.0 + nrm(shape, 0.01)

    x_prompt = nrm((BATCH, SEQ, D_MODEL))
    x_sample = nrm((DEC_BATCH, DEC_SEQ, D_MODEL))
    cache_mla_ckv = nrm((N_EVEN, n_phys, PAGE_SIZE, MLA_KV_RANK))
    cache_mla_kpe = nrm((N_EVEN, n_phys, PAGE_SIZE, MLA_ROPE))
    cache_moba_k = nrm((N_EVEN, n_phys, PAGE_SIZE, MOBA_HEADS, HEAD_DIM))
    cache_moba_v = nrm((N_EVEN, n_phys, PAGE_SIZE, MOBA_HEADS, HEAD_DIM))
    cache_fox_k = nrm((N_ODD, n_phys, PAGE_SIZE, FOX_KV_HEADS, HEAD_DIM))
    cache_fox_v = nrm((N_ODD, n_phys, PAGE_SIZE, FOX_KV_HEADS, HEAD_DIM))
    cache_fox_logf = jax.nn.log_sigmoid(FORGET_BIAS + nrm((N_ODD, n_phys, PAGE_SIZE, FOX_HEADS)))
    page_table = jax.random.permutation(next(ks), n_phys)[:DEC_BATCH * n_pages].reshape(DEC_BATCH, n_pages).astype(jnp.int32)
    return {
        'x_prompt': x_prompt,
        'x_sample': x_sample,
        'cache_mla_ckv': cache_mla_ckv,
        'cache_mla_kpe': cache_mla_kpe,
        'cache_moba_k': cache_moba_k,
        'cache_moba_v': cache_moba_v,
        'cache_fox_k': cache_fox_k,
        'cache_fox_v': cache_fox_v,
        'cache_fox_logf': cache_fox_logf,
        'page_table': page_table,
        'attn_norm': gain((DEPTH, D_MODEL)),
        'ffn_norm': gain((DEPTH, D_MODEL)),
        'final_norm': gain((D_MODEL,)),
        'w_in_even': nrm((N_EVEN, D_MODEL, EVEN_IN), D_MODEL ** -0.5),
        'mla_q_norm': gain((N_EVEN, MLA_Q_RANK)),
        'mla_kv_norm': gain((N_EVEN, MLA_KV_RANK)),
        'mla_w_uq': nrm((N_EVEN, MLA_Q_RANK, MLA_HEADS, MLA_NOPE + MLA_ROPE), MLA_Q_RANK ** -0.5),
        'mla_w_uk': nrm((N_EVEN, MLA_KV_RANK, MLA_HEADS, MLA_NOPE), MLA_KV_RANK ** -0.5),
        'mla_w_uv': nrm((N_EVEN, MLA_KV_RANK, MLA_HEADS, MLA_V), MLA_KV_RANK ** -0.5),
        'w_out_even': nrm((N_EVEN, EVEN_MIX, D_MODEL), EVEN_MIX ** -0.5),
        'w_in_odd': nrm((N_ODD, D_MODEL, ODD_IN), D_MODEL ** -0.5),
        'fox_b_forget': FORGET_BIAS + nrm((N_ODD, FOX_HEADS), 0.1),
        'w_out_odd': nrm((N_ODD, ODD_MIX, D_MODEL), ODD_MIX ** -0.5),
        'ffn_w_gate': nrm((DEPTH, D_MODEL, FFN_HIDDEN), D_MODEL ** -0.5),
        'ffn_w_up': nrm((DEPTH, D_MODEL, FFN_HIDDEN), D_MODEL ** -0.5),
        'ffn_w_down': nrm((DEPTH, FFN_HIDDEN, D_MODEL), FFN_HIDDEN ** -0.5),
    }


def reference(x_prompt, x_sample, cache_mla_ckv, cache_mla_kpe, cache_moba_k, cache_moba_v,
              cache_fox_k, cache_fox_v, cache_fox_logf, page_table,
              attn_norm, ffn_norm, final_norm, w_in_even, mla_q_norm, mla_kv_norm,
              mla_w_uq, mla_w_uk, mla_w_uv, w_out_even, w_in_odd, fox_b_forget, w_out_odd,
              ffn_w_gate, ffn_w_up, ffn_w_down):
    past = page_table.shape[1] * PAGE_SIZE
    pos_p = jnp.arange(x_prompt.shape[1], dtype=jnp.int32)
    pos_s = past + jnp.arange(x_sample.shape[1], dtype=jnp.int32)
    xp, xs = x_prompt, x_sample
    ckv_p, ckv_s, kpe_p, kpe_s, mk_p, mk_s, mv_p, mv_s = [], [], [], [], [], [], [], []
    fk_p, fk_s, fv_p, fv_s, fl_p, fl_s = [], [], [], [], [], []
    for l in range(DEPTH):
        hp = rmsnorm(xp, attn_norm[l])
        hs = rmsnorm(xs, attn_norm[l])
        if l % 2 == 0:
            e = l // 2
            w = (w_in_even[e], mla_q_norm[e], mla_kv_norm[e], mla_w_uq[e], mla_w_uk[e])
            ql, qpe, ckv, kpe, mq, mk, mv = even_project(hp, pos_p, *w)
            yp = even_output(mla_prompt(ql, qpe, ckv, kpe), moba_prompt(mq, mk, mv), mla_w_uv[e], w_out_even[e])
            ckv_p.append(ckv)
            kpe_p.append(kpe)
            mk_p.append(mk.transpose(0, 2, 1, 3))
            mv_p.append(mv.transpose(0, 2, 1, 3))
            ql, qpe, ckv, kpe, mq, mk, mv = even_project(hs, pos_s, *w)
            o_lat = mla_sample(ql, qpe, ckv, kpe, cache_mla_ckv, cache_mla_kpe, e, page_table)
            o_moba = moba_sample(mq, mk, mv, cache_moba_k, cache_moba_v, e, page_table)
            ys = even_output(o_lat, o_moba, mla_w_uv[e], w_out_even[e])
            ckv_s.append(ckv)
            kpe_s.append(kpe)
            mk_s.append(mk.transpose(0, 2, 1, 3))
            mv_s.append(mv.transpose(0, 2, 1, 3))
        else:
            o = l // 2
            q, k, v, lf = fox_project(hp @ w_in_odd[o], fox_b_forget[o])
            yp = fox_output(fox_prompt(q, k, v, lf), w_out_odd[o])
            fk_p.append(k.transpose(0, 2, 1, 3))
            fv_p.append(v.transpose(0, 2, 1, 3))
            fl_p.append(lf)
            q, k, v, lf = fox_project(hs @ w_in_odd[o], fox_b_forget[o])
            ys = fox_output(fox_sample(q, k, v, lf, cache_fox_k, cache_fox_v, cache_fox_logf, o, page_table), w_out_odd[o])
            fk_s.append(k.transpose(0, 2, 1, 3))
            fv_s.append(v.transpose(0, 2, 1, 3))
            fl_s.append(lf)
        xp = swiglu_block(xp + yp, ffn_norm[l], ffn_w_gate[l], ffn_w_up[l], ffn_w_down[l])
        xs = swiglu_block(xs + ys, ffn_norm[l], ffn_w_gate[l], ffn_w_up[l], ffn_w_down[l])
    y_prompt = rmsnorm(xp, final_norm)
    y_sample = rmsnorm(xs, final_norm)
    mla_ckv_p, mla_ckv_s = jnp.stack(ckv_p), jnp.stack(ckv_s)
    mla_kpe_p, mla_kpe_s = jnp.stack(kpe_p), jnp.stack(kpe_s)
    moba_k_p, moba_k_s = jnp.stack(mk_p), jnp.stack(mk_s)
    moba_v_p, moba_v_s = jnp.stack(mv_p), jnp.stack(mv_s)
    fox_k_p, fox_k_s = jnp.stack(fk_p), jnp.stack(fk_s)
    fox_v_p, fox_v_s = jnp.stack(fv_p), jnp.stack(fv_s)
    fox_logf_p, fox_logf_s = jnp.stack(fl_p), jnp.stack(fl_s)
    return (y_prompt, y_sample, mla_ckv_p, mla_ckv_s, mla_kpe_p, mla_kpe_s, moba_k_p, moba_k_s,
            moba_v_p, moba_v_s, fox_k_p, fox_k_s, fox_v_p, fox_v_s, fox_logf_p, fox_logf_s)
```

```python
import functools

import jax
import jax.numpy as jnp
from jax import lax
from jax.experimental import pallas as pl
from jax.experimental.pallas import tpu as pltpu

F32 = jnp.float32
BF16 = jnp.bfloat16

D_MODEL = 1024
HEAD_DIM = 64
PAGE = 128
Q_BLOCK = 128
RMS_EPS = 1e-6
MLA_HEADS = 8
MLA_Q_RANK = 256
MLA_KV_RANK = 128
MLA_NOPE = 64
MLA_ROPE = 32
MLA_V = 64
ROPE_THETA = 10000.0
MLA_SCALE = (MLA_NOPE + MLA_ROPE) ** -0.5
MOBA_HEADS = 8
MOBA_BLOCK = 256
MOBA_TOPK = 3
MOBA_SCALE = HEAD_DIM ** -0.5
FOX_HEADS = 16
FOX_KV_HEADS = 8
FOX_SCALE = HEAD_DIM ** -0.5
FFN_HIDDEN = 2816
FFN_CHUNK = 1408
LANES = 128
NEG = -1e30
VMEM_LIMIT = 56 * 1024 * 1024

HI = lax.Precision.HIGHEST


def _nt(a, b, precision=None):
    return lax.dot_general(a, b, (((1,), (1,)), ((), ())), precision=precision,
                           preferred_element_type=F32)


def _dot(a, b, precision=None):
    return jnp.dot(a, b, precision=precision, preferred_element_type=F32)


def _rms(x, g):
    return x * lax.rsqrt(jnp.mean(x * x, axis=-1, keepdims=True) + RMS_EPS) * g


def _params(*sem):
    return pltpu.CompilerParams(dimension_semantics=sem, vmem_limit_bytes=VMEM_LIMIT)


def _const_spec(shape):
    nd = len(shape)
    return pl.BlockSpec(shape, lambda *a: (0,) * nd, pipeline_mode=pl.Buffered(1))


def _proj_even_kernel(x_ref, g_ref, win_ref, qn_ref, kvn_ref, wuq_ref, wuk_ref, cs_ref, sn_ref,
                      qcat_ref, kcat_ref, ckv_ref, kpe_ref, mq_ref, mk_ref, mv_ref):
    h = _rms(x_ref[...], g_ref[...]).astype(BF16)

    def mm(lo, hi):
        return _dot(h, win_ref[:, lo:hi])

    cq = _rms(mm(0, 256), qn_ref[...]).astype(BF16)
    ckv = _rms(mm(256, 384), kvn_ref[...])
    cs = cs_ref[...]
    sn = sn_ref[...]
    kpe = mm(384, 512) * cs + mm(512, 640) * sn
    ckv_ref[...] = ckv
    kpe_ref[...] = kpe[:, :MLA_ROPE]
    kcat_ref[:, 0:128] = ckv.astype(kcat_ref.dtype)
    kcat_ref[:, 128:256] = kpe.astype(kcat_ref.dtype)
    mq_ref[...] = mm(640, 1152)
    mk_ref[...] = mm(1152, 1664)
    mv_ref[...] = mm(1664, 2176)
    qn = _dot(cq, wuq_ref[:, 0:512]).astype(BF16)
    for p in range(MLA_HEADS // 2):
        ql = _dot(qn[:, 128 * p:128 * p + 128], wuk_ref[p])
        for hh in range(2):
            hd = 2 * p + hh
            qpe = (_dot(cq, wuq_ref[:, 512 + 128 * hd:640 + 128 * hd]) * cs
                   + _dot(cq, wuq_ref[:, 1536 + 128 * hd:1664 + 128 * hd]) * sn)
            qcat_ref[:, 256 * hd:256 * hd + 128] = ql[:, 128 * hh:128 * hh + 128].astype(qcat_ref.dtype)
            qcat_ref[:, 256 * hd + 128:256 * hd + 256] = qpe.astype(qcat_ref.dtype)


def _proj_even(x, g, w_in, qn, kvn, w_uq, w_uk, cs, sn, tm, act_dtype):
    m = x.shape[0]
    row = lambda w: pl.BlockSpec((tm, w), lambda i: (i, 0))
    out_shape = (
        jax.ShapeDtypeStruct((m, 2048), act_dtype),
        jax.ShapeDtypeStruct((m, 256), act_dtype),
        jax.ShapeDtypeStruct((m, 128), F32),
        jax.ShapeDtypeStruct((m, 32), F32),
        jax.ShapeDtypeStruct((m, 512), F32),
        jax.ShapeDtypeStruct((m, 512), F32),
        jax.ShapeDtypeStruct((m, 512), F32),
    )
    return pl.pallas_call(
        _proj_even_kernel,
        grid=(m // tm,),
        in_specs=[row(D_MODEL), _const_spec((1, D_MODEL)), _const_spec(w_in.shape), _const_spec((1, 256)),
                  _const_spec((1, 128)), _const_spec(w_uq.shape), _const_spec(w_uk.shape), row(128), row(128)],
        out_specs=(row(2048), row(256), row(128), row(32), row(512), row(512), row(512)),
        out_shape=out_shape,
        compiler_params=_params("parallel"),
        name="proj_even",
    )(x, g, w_in, qn, kvn, w_uq, w_uk, cs, sn)


def _proj_odd_kernel(x_ref, g_ref, win_ref, bf_ref, q_ref, k_ref, v_ref, lf_ref):
    h = _rms(x_ref[...], g_ref[...]).astype(BF16)

    def mm(lo, hi):
        return _dot(h, win_ref[:, lo:hi])

    q_ref[...] = mm(0, 1024).astype(q_ref.dtype)
    k_ref[...] = mm(1024, 1536)
    v_ref[...] = mm(1536, 2048)
    zf = mm(2048, 2176)[:, :FOX_HEADS] + bf_ref[...]
    lf_ref[...] = -(jnp.maximum(-zf, 0.0) + jnp.log1p(jnp.exp(-jnp.abs(zf))))


def _proj_odd(x, g, w_in, b_f, tm, act_dtype):
    m = x.shape[0]
    row = lambda w: pl.BlockSpec((tm, w), lambda i: (i, 0))
    out_shape = (
        jax.ShapeDtypeStruct((m, 1024), act_dtype),
        jax.ShapeDtypeStruct((m, 512), F32),
        jax.ShapeDtypeStruct((m, 512), F32),
        jax.ShapeDtypeStruct((m, FOX_HEADS), F32),
    )
    return pl.pallas_call(
        _proj_odd_kernel,
        grid=(m // tm,),
        in_specs=[row(D_MODEL), _const_spec((1, D_MODEL)), _const_spec(w_in.shape), _const_spec((1, FOX_HEADS))],
        out_specs=(row(1024), row(512), row(512), row(FOX_HEADS)),
        out_shape=out_shape,
        compiler_params=_params("parallel"),
        name="proj_odd",
    )(x, g, w_in, b_f)


def _out_ffn_kernel(x_ref, a_ref, b_ref, wo_ref, g_ref, wg_ref, wu_ref, wd_ref, fn_ref, o_ref, *, final):
    x1 = (x_ref[...] + _dot(a_ref[...].astype(BF16), wo_ref[0:512, :])
          + _dot(b_ref[...].astype(BF16), wo_ref[512:1024, :]))
    h = _rms(x1, g_ref[...]).astype(BF16)
    acc = jnp.zeros_like(x1)
    for c in range(FFN_HIDDEN // FFN_CHUNK):
        lo = c * FFN_CHUNK
        gate = _dot(h, wg_ref[:, lo:lo + FFN_CHUNK])
        up = _dot(h, wu_ref[:, lo:lo + FFN_CHUNK])
        act = (gate / (1.0 + jnp.exp(-gate))) * up
        acc = acc + _dot(act.astype(BF16), wd_ref[lo:lo + FFN_CHUNK, :])
    out = x1 + acc
    if final:
        out = _rms(out, fn_ref[...])
    o_ref[...] = out


def _out_ffn(x, mix_a, mix_b, w_out, g, w_gate, w_up, w_down, final_norm, tm, final):
    m = x.shape[0]
    row = lambda w: pl.BlockSpec((tm, w), lambda i: (i, 0))
    return pl.pallas_call(
        functools.partial(_out_ffn_kernel, final=final),
        grid=(m // tm,),
        in_specs=[row(D_MODEL), row(512), row(512), _const_spec(w_out.shape), _const_spec((1, D_MODEL)),
                  _const_spec(w_gate.shape), _const_spec(w_up.shape), _const_spec(w_down.shape),
                  _const_spec((1, D_MODEL))],
        out_specs=row(D_MODEL),
        out_shape=jax.ShapeDtypeStruct((m, D_MODEL), F32),
        compiler_params=_params("parallel"),
        name="out_ffn",
    )(x, mix_a, mix_b, w_out, g, w_gate, w_up, w_down, final_norm)


def _softmax_step(s, m_ref, l_ref, acc_ref, v_bf16):
    m_old = m_ref[...]
    m_new = jnp.maximum(m_old, jnp.max(s, axis=-1, keepdims=True))
    alpha = jnp.exp(m_old - m_new)
    p = jnp.exp(s - m_new)
    l_ref[...] = alpha * l_ref[...] + jnp.sum(p, axis=-1, keepdims=True)
    acc_ref[...] = alpha * acc_ref[...] + _dot(p.astype(BF16), v_bf16)
    m_ref[...] = m_new


def _init_stats(m_ref, l_ref, acc_ref):
    m_ref[...] = jnp.full(m_ref.shape, NEG, F32)
    l_ref[...] = jnp.zeros(l_ref.shape, F32)
    acc_ref[...] = jnp.zeros(acc_ref.shape, F32)


def _mla_prompt_kernel(q_ref, k_ref, wuv_ref, o_ref, m_ref, l_ref, acc_ref, *, tq, tk):
    i = pl.program_id(1)
    nh = MLA_HEADS
    q = q_ref[...]
    qs = jnp.concatenate([q[:, 256 * h:256 * h + 256] for h in range(nh)], axis=0)
    qpos = i * tq + lax.broadcasted_iota(jnp.int32, (nh * tq, 1), 0) % tq
    _init_stats(m_ref, l_ref, acc_ref)
    n_kv = (i * tq + tq - 1) // tk + 1

    def body(j, carry):
        kb = k_ref[pl.ds(pl.multiple_of(j * tk, tk), tk), :]
        s = _nt(qs, kb) * MLA_SCALE
        kpos = j * tk + lax.broadcasted_iota(jnp.int32, (1, tk), 1)
        s = jnp.where(kpos <= qpos, s, NEG)
        _softmax_step(s, m_ref, l_ref, acc_ref, kb[:, 0:MLA_KV_RANK])
        return carry

    lax.fori_loop(0, n_kv, body, 0)
    o = acc_ref[...] / l_ref[...]
    for p in range(nh // 2):
        pair = jnp.concatenate([o[(2 * p) * tq:(2 * p + 1) * tq], o[(2 * p + 1) * tq:(2 * p + 2) * tq]], axis=1)
        o_ref[:, 128 * p:128 * p + 128] = _dot(pair.astype(BF16), wuv_ref[p]).astype(o_ref.dtype)


def _mla_prompt(qcat, kcat, w_uv_bd, nb, seq):
    tq, tk = Q_BLOCK, 256
    nq = seq // tq
    return pl.pallas_call(
        functools.partial(_mla_prompt_kernel, tq=tq, tk=tk),
        grid=(nb, nq),
        in_specs=[pl.BlockSpec((tq, 2048), lambda b, i: (b * nq + i, 0)),
                  pl.BlockSpec((seq, 256), lambda b, i: (b, 0)),
                  _const_spec(w_uv_bd.shape)],
        out_specs=pl.BlockSpec((tq, 512), lambda b, i: (b * nq + i, 0)),
        out_shape=jax.ShapeDtypeStruct((nb * seq, 512), BF16),
        scratch_shapes=[pltpu.VMEM((MLA_HEADS * tq, 1), F32), pltpu.VMEM((MLA_HEADS * tq, 1), F32),
                        pltpu.VMEM((MLA_HEADS * tq, MLA_KV_RANK), F32)],
        compiler_params=_params("parallel", "arbitrary"),
        name="mla_prompt",
    )(qcat, kcat, w_uv_bd)


def _top3_mask(scores, n_valid):
    rows, nblk = scores.shape
    lane = lax.broadcasted_iota(jnp.int32, (rows, nblk), 1).astype(F32)
    avail = jnp.where(lane < n_valid, 1.0, 0.0)
    sel = jnp.zeros((rows, nblk), F32)
    for _ in range(MOBA_TOPK):
        work = jnp.where(avail > 0.0, scores, -jnp.inf)
        mx = jnp.max(work, axis=-1, keepdims=True)
        cand = jnp.where(work == mx, jnp.where(avail > 0.0, lane, float(nblk)), float(nblk))
        idx = jnp.min(cand, axis=-1, keepdims=True)
        pick = lane == idx
        sel = jnp.where(pick, 1.0, sel)
        avail = jnp.where(pick, 0.0, avail)
    return sel


def _moba_prompt_kernel(q_ref, k_ref, v_ref, slope_ref, o_ref, mean_ref, m_ref, l_ref, acc_ref, *, tq, nblk):
    i = pl.program_id(2)
    blk = MOBA_BLOCK

    @pl.when(i == 0)
    def _():
        for j in range(nblk):
            mean_ref[j:j + 1, :] = jnp.sum(k_ref[j * blk:(j + 1) * blk, :], axis=0, keepdims=True) * (1.0 / blk)

    c = (i * tq) // blk
    q = q_ref[...]
    lane = lax.broadcasted_iota(jnp.int32, (tq, LANES), 1)
    qbd = jnp.concatenate([jnp.where(lane < HEAD_DIM, q, 0.0), jnp.where(lane >= HEAD_DIM, q, 0.0)], axis=0)
    qbd16 = qbd.astype(BF16)
    sel = _top3_mask(_nt(qbd, mean_ref[...], precision=HI), c)
    sel_lane = lax.broadcasted_iota(jnp.int32, (2 * tq, nblk), 1)
    slope = slope_ref[0]
    qpos = i * tq + lax.broadcasted_iota(jnp.int32, (2 * tq, 1), 0) % tq
    _init_stats(m_ref, l_ref, acc_ref)

    def block(j, own):
        start = pl.multiple_of(j * blk, blk)
        kb = k_ref[pl.ds(start, blk), :].astype(BF16)
        vb = v_ref[pl.ds(start, blk), :].astype(BF16)
        kpos = j * blk + lax.broadcasted_iota(jnp.int32, (1, blk), 1)
        s = _nt(qbd16, kb) * MOBA_SCALE - slope * (qpos - kpos).astype(F32)
        if own:
            keep = kpos <= qpos
        else:
            keep = jnp.sum(jnp.where(sel_lane == j, sel, 0.0), axis=-1, keepdims=True) > 0.0
        _softmax_step(jnp.where(keep, s, NEG), m_ref, l_ref, acc_ref, vb)

    block(c, True)

    def body(j, carry):
        block(j, False)
        return carry

    lax.fori_loop(0, c, body, 0)
    o = acc_ref[...] / l_ref[...]
    o_ref[...] = jnp.where(lane < HEAD_DIM, o[0:tq], o[tq:2 * tq]).astype(o_ref.dtype)


def _moba_prompt(mq, mk, mv, slopes, nb, seq):
    tq = Q_BLOCK
    nq = seq // tq
    nblk = seq // MOBA_BLOCK
    npair = MOBA_HEADS // 2
    return pl.pallas_call(
        functools.partial(_moba_prompt_kernel, tq=tq, nblk=nblk),
        grid=(nb, npair, nq),
        in_specs=[pl.BlockSpec((tq, LANES), lambda b, p, i: (b * nq + i, p)),
                  pl.BlockSpec((seq, LANES), lambda b, p, i: (b, p)),
                  pl.BlockSpec((seq, LANES), lambda b, p, i: (b, p)),
                  pl.BlockSpec((1, 2 * tq, 1), lambda b, p, i: (p, 0, 0))],
        out_specs=pl.BlockSpec((tq, LANES), lambda b, p, i: (b * nq + i, p)),
        out_shape=jax.ShapeDtypeStruct((nb * seq, 512), BF16),
        scratch_shapes=[pltpu.VMEM((nblk, LANES), F32), pltpu.VMEM((2 * tq, 1), F32), pltpu.VMEM((2 * tq, 1), F32),
                        pltpu.VMEM((2 * tq, LANES), F32)],
        compiler_params=_params("parallel", "parallel", "arbitrary"),
        name="moba_prompt",
    )(mq, mk, mv, slopes)


def _fox_cumsum_kernel(lf_ref, lft_ref, f_ref, ft_ref, *, seq, ch):
    r = lax.broadcasted_iota(jnp.int32, (ch, ch), 0)
    c = lax.broadcasted_iota(jnp.int32, (ch, ch), 1)
    lower = (c <= r).astype(F32)
    upper = (r <= c).astype(F32)
    carry = jnp.zeros((1, FOX_HEADS), F32)
    carry_t = jnp.zeros((FOX_HEADS, 1), F32)
    for j in range(seq // ch):
        f = _dot(lower, lf_ref[j * ch:(j + 1) * ch, :], precision=HI) + carry
        f_ref[j * ch:(j + 1) * ch, :] = f
        carry = f[ch - 1:ch, :]
        ft = _dot(lft_ref[0, :, j * ch:(j + 1) * ch], upper, precision=HI) + carry_t
        ft_ref[0, :, j * ch:(j + 1) * ch] = ft
        carry_t = ft[:, ch - 1:ch]


def _fox_cumsum(logf, logf_t, nb, seq):
    return pl.pallas_call(
        functools.partial(_fox_cumsum_kernel, seq=seq, ch=256),
        grid=(nb,),
        in_specs=[pl.BlockSpec((seq, FOX_HEADS), lambda b: (b, 0)),
                  pl.BlockSpec((1, FOX_HEADS, seq), lambda b: (b, 0, 0))],
        out_specs=(pl.BlockSpec((seq, FOX_HEADS), lambda b: (b, 0)),
                   pl.BlockSpec((1, FOX_HEADS, seq), lambda b: (b, 0, 0))),
        out_shape=(jax.ShapeDtypeStruct((nb * seq, FOX_HEADS), F32),
                   jax.ShapeDtypeStruct((nb, FOX_HEADS, seq), F32)),
        compiler_params=_params("parallel"),
        name="fox_cumsum",
    )(logf, logf_t)


def _fox_prompt_kernel(q0_ref, q1_ref, k_ref, v_ref, fq_ref, fk_ref, o_ref, m_ref, l_ref, acc_ref, *, tq, tk):
    i = pl.program_id(2)
    lane = lax.broadcasted_iota(jnp.int32, (tq, LANES), 1)
    lo = lane < HEAD_DIM
    q0 = q0_ref[...]
    q1 = q1_ref[...]
    zero = jnp.zeros_like(q0)
    qbd = jnp.concatenate([jnp.where(lo, q0, zero), jnp.where(lo, q1, zero),
                           jnp.where(lo, zero, q0), jnp.where(lo, zero, q1)], axis=0)
    fq = jnp.concatenate([fq_ref[0, :, r:r + 1] for r in range(4)], axis=0)
    qpos = i * tq + lax.broadcasted_iota(jnp.int32, (4 * tq, 1), 0) % tq
    _init_stats(m_ref, l_ref, acc_ref)
    n_kv = (i * tq + tq - 1) // tk + 1

    def body(j, carry):
        start = pl.multiple_of(j * tk, tk)
        kb = k_ref[pl.ds(start, tk), :].astype(BF16)
        vb = v_ref[pl.ds(start, tk), :].astype(BF16)
        fk = jnp.concatenate([jnp.broadcast_to(fk_ref[0, 0, r:r + 1, pl.ds(start, tk)], (tq, tk))
                              for r in range(4)], axis=0)
        s = _nt(qbd, kb) * FOX_SCALE + (fq - fk)
        kpos = j * tk + lax.broadcasted_iota(jnp.int32, (1, tk), 1)
        _softmax_step(jnp.where(kpos <= qpos, s, NEG), m_ref, l_ref, acc_ref, vb)
        return carry

    lax.fori_loop(0, n_kv, body, 0)
    o = acc_ref[...] / l_ref[...]
    o_ref[0] = jnp.where(lo, o[0:tq], o[2 * tq:3 * tq]).astype(o_ref.dtype)
    o_ref[1] = jnp.where(lo, o[tq:2 * tq], o[3 * tq:4 * tq]).astype(o_ref.dtype)


def _fox_prompt(q, k, v, fq, fk, nb, seq):
    tq, tk = Q_BLOCK, 256
    nq = seq // tq
    npair = FOX_KV_HEADS // 2
    return pl.pallas_call(
        functools.partial(_fox_prompt_kernel, tq=tq, tk=tk),
        grid=(nb, npair, nq),
        in_specs=[pl.BlockSpec((tq, LANES), lambda b, p, i: (b * nq + i, p)),
                  pl.BlockSpec((tq, LANES), lambda b, p, i: (b * nq + i, npair + p)),
                  pl.BlockSpec((seq, LANES), lambda b, p, i: (b, p)),
                  pl.BlockSpec((seq, LANES), lambda b, p, i: (b, p)),
                  pl.BlockSpec((1, tq, 4), lambda b, p, i: (p, b * nq + i, 0)),
                  pl.BlockSpec((1, 1, 4, seq), lambda b, p, i: (b, p, 0, 0))],
        out_specs=pl.BlockSpec((2, tq, LANES), lambda b, p, i: (0, b * nq + i, p)),
        out_shape=jax.ShapeDtypeStruct((2, nb * seq, 512), BF16),
        scratch_shapes=[pltpu.VMEM((4 * tq, 1), F32), pltpu.VMEM((4 * tq, 1), F32),
                        pltpu.VMEM((4 * tq, LANES), F32)],
        compiler_params=_params("parallel", "parallel", "arbitrary"),
        name="fox_prompt",
    )(q, q, k, v, fq, fk)


def _page_specs(shape, npg, layer, reverse_chunks=None):
    specs = []
    for pg in range(npg):
        if reverse_chunks is None:
            imap = lambda b, c, pt, pg=pg: (layer, pt[b, c * npg + pg], 0, 0)
        else:
            imap = lambda b, c, pt, pg=pg: (layer, pt[b, (reverse_chunks - 1 - c) * npg + pg], 0, 0)
        specs.append(pl.BlockSpec((1, 1, PAGE, shape), imap))
    return specs


def _mla_decode_kernel(pt_ref, q_ref, kn_ref, wuv_ref, *rest, npg, t_new):
    ckv_refs = rest[:npg]
    kpe_refs = rest[npg:2 * npg]
    o_ref, m_ref, l_ref, acc_ref = rest[2 * npg:]
    c = pl.program_id(1)
    nh = MLA_HEADS
    rows = nh * t_new
    q = q_ref[...]
    qs = jnp.concatenate([q[:, 256 * h:256 * h + 256] for h in range(nh)], axis=0)
    qs16 = qs.astype(BF16)

    @pl.when(c == 0)
    def _():
        kn = kn_ref[...]
        s = _nt(qs16, kn.astype(BF16)) * MLA_SCALE
        tq = lax.broadcasted_iota(jnp.int32, (rows, t_new), 0) % t_new
        tk = lax.broadcasted_iota(jnp.int32, (rows, t_new), 1)
        s = jnp.where(tk <= tq, s, NEG)
        m = jnp.max(s, axis=-1, keepdims=True)
        p = jnp.exp(s - m)
        acc = jnp.zeros((rows, MLA_KV_RANK), F32)
        for t in range(t_new):
            acc = acc + p[:, t:t + 1] * kn[t:t + 1, 0:MLA_KV_RANK]
        m_ref[...] = m
        l_ref[...] = jnp.sum(p, axis=-1, keepdims=True)
        acc_ref[...] = acc

    ck = jnp.concatenate([r[0, 0] for r in ckv_refs], axis=0).astype(BF16)
    kp = jnp.concatenate([r[0, 0] for r in kpe_refs], axis=0).astype(BF16)
    s = (_nt(qs16[:, 0:MLA_KV_RANK], ck) + _nt(qs16[:, MLA_KV_RANK:MLA_KV_RANK + MLA_ROPE], kp)) * MLA_SCALE
    _softmax_step(s, m_ref, l_ref, acc_ref, ck)

    @pl.when(c == pl.num_programs(1) - 1)
    def _():
        o = acc_ref[...] / l_ref[...]
        for p in range(nh // 2):
            pair = jnp.concatenate([o[(2 * p) * t_new:(2 * p + 1) * t_new],
                                    o[(2 * p + 1) * t_new:(2 * p + 2) * t_new]], axis=1)
            o_ref[:, 128 * p:128 * p + 128] = _dot(pair.astype(BF16), wuv_ref[p]).astype(o_ref.dtype)


def _mla_decode(qcat, kcat, w_uv_bd, pool_ckv, pool_kpe, layer, page_table, t_new):
    nreq, n_pages = page_table.shape
    npg = min(16, n_pages)
    nchunk = n_pages // npg
    rows = MLA_HEADS * t_new
    grid_spec = pltpu.PrefetchScalarGridSpec(
        num_scalar_prefetch=1,
        grid=(nreq, nchunk),
        in_specs=[pl.BlockSpec((t_new, 2048), lambda b, c, pt: (b, 0)),
                  pl.BlockSpec((t_new, 256), lambda b, c, pt: (b, 0)),
                  pl.BlockSpec(w_uv_bd.shape, lambda b, c, pt: (0, 0, 0))]
        + _page_specs(MLA_KV_RANK, npg, layer) + _page_specs(MLA_ROPE, npg, layer),
        out_specs=pl.BlockSpec((t_new, 512), lambda b, c, pt: (b, 0)),
        scratch_shapes=[pltpu.VMEM((rows, 1), F32), pltpu.VMEM((rows, 1), F32),
                        pltpu.VMEM((rows, MLA_KV_RANK), F32)],
    )
    return pl.pallas_call(
        functools.partial(_mla_decode_kernel, npg=npg, t_new=t_new),
        grid_spec=grid_spec,
        out_shape=jax.ShapeDtypeStruct((nreq * t_new, 512), F32),
        compiler_params=_params("parallel", "arbitrary"),
        name="mla_decode",
    )(page_table, qcat, kcat, w_uv_bd, *([pool_ckv] * npg), *([pool_kpe] * npg))


def _moba_means_kernel(pt_ref, *rest, npg):
    k_refs = rest[:npg]
    o_ref = rest[npg]
    ppb = MOBA_BLOCK // PAGE
    for j in range(npg // ppb):
        tot = jnp.zeros((1, 512), F32)
        for pg in range(ppb):
            tot = tot + jnp.sum(k_refs[j * ppb + pg][0, 0], axis=0, keepdims=True)
        o_ref[0, j:j + 1, :] = tot * (1.0 / MOBA_BLOCK)


def _moba_means(pool_k, layer, page_table):
    nreq, n_pages = page_table.shape
    npg = min(16, n_pages)
    nchunk = n_pages // npg
    ppb = MOBA_BLOCK // PAGE
    grid_spec = pltpu.PrefetchScalarGridSpec(
        num_scalar_prefetch=1,
        grid=(nreq, nchunk),
        in_specs=_page_specs(512, npg, layer),
        out_specs=pl.BlockSpec((1, npg // ppb, 512), lambda b, c, pt: (b, c, 0)),
    )
    return pl.pallas_call(
        functools.partial(_moba_means_kernel, npg=npg),
        grid_spec=grid_spec,
        out_shape=jax.ShapeDtypeStruct((nreq, n_pages // ppb, 512), F32),
        compiler_params=_params("parallel", "arbitrary"),
        name="moba_means",
    )(page_table, *([pool_k] * npg))


def _head_mask(rows_per_group, n_groups, head_of_group):
    rows = n_groups * rows_per_group
    grp = lax.broadcasted_iota(jnp.int32, (rows, 512), 0) // rows_per_group
    blk = lax.broadcasted_iota(jnp.int32, (rows, 512), 1) // HEAD_DIM
    return blk == head_of_group(grp)


def _new_token_init(qbd16, kn, vn, bias_new, m_ref, l_ref, acc_ref, scale, t_new):
    rows = qbd16.shape[0]
    s = _nt(qbd16, kn.astype(BF16)) * scale + bias_new
    tq = lax.broadcasted_iota(jnp.int32, (rows, t_new), 0) % t_new
    tk = lax.broadcasted_iota(jnp.int32, (rows, t_new), 1)
    s = jnp.where(tk <= tq, s, NEG)
    m = jnp.max(s, axis=-1, keepdims=True)
    p = jnp.exp(s - m)
    acc = jnp.zeros((rows, 512), F32)
    for t in range(t_new):
        acc = acc + p[:, t:t + 1] * vn[t:t + 1, :]
    m_ref[...] = m
    l_ref[...] = jnp.sum(p, axis=-1, keepdims=True)
    acc_ref[...] = acc


def _fox_decode_kernel(pt_ref, q_ref, kn_ref, vn_ref, lfn_ref, *rest, npg, t_new):
    k_refs = rest[:npg]
    v_refs = rest[npg:2 * npg]
    lf_refs = rest[2 * npg:3 * npg]
    o_ref, m_ref, l_ref, acc_ref, carry_ref, gq_ref = rest[3 * npg:]
    c = pl.program_id(1)
    nhq = FOX_HEADS
    rows = nhq * t_new
    tk = npg * PAGE
    hmask = _head_mask(t_new, nhq, lambda grp: grp // 2)
    q = q_ref[...]
    qrep = jnp.concatenate([q[:, 512 * (hq % 2):512 * (hq % 2) + 512] for hq in range(nhq)], axis=0)
    qbd16 = jnp.where(hmask, qrep, 0.0).astype(BF16)
    eye = (lax.broadcasted_iota(jnp.int32, (nhq, nhq), 0)
           == lax.broadcasted_iota(jnp.int32, (nhq, nhq), 1)).astype(F32)

    def expand(xt):
        n = xt.shape[1]
        return jnp.concatenate([jnp.broadcast_to(xt[hq:hq + 1, :], (t_new, n)) for hq in range(nhq)], axis=0)

    @pl.when(c == 0)
    def _():
        lfn = lfn_ref[...]
        tri = (lax.broadcasted_iota(jnp.int32, (t_new, t_new), 1)
               <= lax.broadcasted_iota(jnp.int32, (t_new, t_new), 0)).astype(F32)
        gc = _dot(tri, lfn, precision=HI)
        g = expand(_nt(eye, gc, precision=HI))
        tq = lax.broadcasted_iota(jnp.int32, (rows, t_new), 0) % t_new
        tl = lax.broadcasted_iota(jnp.int32, (rows, t_new), 1)
        gq = jnp.sum(jnp.where(tl == tq, g, 0.0), axis=-1, keepdims=True)
        gq_ref[...] = gq
        carry_ref[...] = jnp.zeros(carry_ref.shape, F32)
        _new_token_init(qbd16, kn_ref[...], vn_ref[...], gq - g, m_ref, l_ref, acc_ref, FOX_SCALE, t_new)

    lf = jnp.concatenate([r[0, 0] for r in lf_refs], axis=0)
    lft = _nt(eye, lf, precision=HI)
    lane = lax.broadcasted_iota(jnp.int32, (nhq, tk), 1)
    incl = lft
    sh = 1
    while sh < tk:
        incl = incl + jnp.where(lane < tk - sh, pltpu.roll(incl, tk - sh, axis=1), 0.0)
        sh *= 2
    suffix = incl - lft + carry_ref[...]
    carry_ref[...] = carry_ref[...] + incl[:, 0:1]
    kb = jnp.concatenate([r[0, 0] for r in k_refs], axis=0).astype(BF16)
    vb = jnp.concatenate([r[0, 0] for r in v_refs], axis=0).astype(BF16)
    s = _nt(qbd16, kb) * FOX_SCALE + (expand(suffix) + gq_ref[...])
    _softmax_step(s, m_ref, l_ref, acc_ref, vb)

    @pl.when(c == pl.num_programs(1) - 1)
    def _():
        o = jnp.where(hmask, acc_ref[...] / l_ref[...], 0.0)
        for g in range(2):
            tot = jnp.zeros((t_new, 512), F32)
            for kvh in range(FOX_KV_HEADS):
                r0 = (kvh * 2 + g) * t_new
                tot = tot + o[r0:r0 + t_new, :]
            o_ref[g] = tot


def _fox_decode(q, k_new, v_new, lf_new, pool_k, pool_v, pool_lf, layer, page_table, t_new):
    nreq, n_pages = page_table.shape
    npg = min(8, n_pages)
    nchunk = n_pages // npg
    rows = FOX_HEADS * t_new
    new = lambda w: pl.BlockSpec((t_new, w), lambda b, c, pt: (b, 0))
    grid_spec = pltpu.PrefetchScalarGridSpec(
        num_scalar_prefetch=1,
        grid=(nreq, nchunk),
        in_specs=[new(1024), new(512), new(512), new(FOX_HEADS)]
        + _page_specs(512, npg, layer, nchunk) + _page_specs(512, npg, layer, nchunk)
        + _page_specs(FOX_HEADS, npg, layer, nchunk),
        out_specs=pl.BlockSpec((2, t_new, 512), lambda b, c, pt: (0, b, 0)),
        scratch_shapes=[pltpu.VMEM((rows, 1), F32), pltpu.VMEM((rows, 1), F32), pltpu.VMEM((rows, 512), F32),
                        pltpu.VMEM((FOX_HEADS, 1), F32), pltpu.VMEM((rows, 1), F32)],
    )
    return pl.pallas_call(
        functools.partial(_fox_decode_kernel, npg=npg, t_new=t_new),
        grid_spec=grid_spec,
        out_shape=jax.ShapeDtypeStruct((2, nreq * t_new, 512), F32),
        compiler_params=_params("parallel", "arbitrary"),
        name="fox_decode",
    )(page_table, q, k_new, v_new, lf_new, *([pool_k] * npg), *([pool_v] * npg), *([pool_lf] * npg))


def _moba_decode_kernel(pt_ref, q_ref, kn_ref, vn_ref, mean_ref, slope_ref, *rest, npg, t_new, past):
    k_refs = rest[:npg]
    v_refs = rest[npg:2 * npg]
    o_ref, m_ref, l_ref, acc_ref, sel_ref = rest[2 * npg:]
    c = pl.program_id(1)
    nh = MOBA_HEADS
    rows = nh * t_new
    tk = npg * PAGE
    bpc = tk // MOBA_BLOCK
    nblk = mean_ref.shape[1]
    hmask = _head_mask(t_new, nh, lambda grp: grp)
    q = q_ref[...]
    qbd = jnp.where(hmask, jnp.concatenate([q] * nh, axis=0), 0.0)
    qbd16 = qbd.astype(BF16)
    slope = slope_ref[...]
    qpos = past + lax.broadcasted_iota(jnp.int32, (rows, 1), 0) % t_new

    @pl.when(c == 0)
    def _():
        sel_ref[...] = _top3_mask(_nt(qbd, mean_ref[0], precision=HI), nblk)
        tl = lax.broadcasted_iota(jnp.int32, (1, t_new), 1)
        bias_new = -slope * (qpos - (past + tl)).astype(F32)
        _new_token_init(qbd16, kn_ref[...], vn_ref[...], bias_new, m_ref, l_ref, acc_ref, MOBA_SCALE, t_new)

    kb = jnp.concatenate([r[0, 0] for r in k_refs], axis=0).astype(BF16)
    vb = jnp.concatenate([r[0, 0] for r in v_refs], axis=0).astype(BF16)
    lane = lax.broadcasted_iota(jnp.int32, (1, tk), 1)
    kpos = c * tk + lane
    s = _nt(qbd16, kb) * MOBA_SCALE - slope * (qpos - kpos).astype(F32)
    sel = sel_ref[...]
    sel_lane = lax.broadcasted_iota(jnp.int32, (rows, nblk), 1)
    keep = jnp.zeros((rows, tk), F32)
    for j in range(bpc):
        chosen = jnp.sum(jnp.where(sel_lane == c * bpc + j, sel, 0.0), axis=-1, keepdims=True)
        keep = jnp.where(lane // MOBA_BLOCK == j, chosen, keep)
    _softmax_step(jnp.where(keep > 0.0, s, NEG), m_ref, l_ref, acc_ref, vb)

    @pl.when(c == pl.num_programs(1) - 1)
    def _():
        o = jnp.where(hmask, acc_ref[...] / l_ref[...], 0.0)
        tot = jnp.zeros((t_new, 512), F32)
        for h in range(nh):
            tot = tot + o[h * t_new:(h + 1) * t_new, :]
        o_ref[...] = tot


def _moba_decode(q, k_new, v_new, means, slopes, pool_k, pool_v, layer, page_table, t_new):
    nreq, n_pages = page_table.shape
    npg = min(8, n_pages)
    nchunk = n_pages // npg
    rows = MOBA_HEADS * t_new
    nblk = means.shape[1]
    new = lambda w: pl.BlockSpec((t_new, w), lambda b, c, pt: (b, 0))
    grid_spec = pltpu.PrefetchScalarGridSpec(
        num_scalar_prefetch=1,
        grid=(nreq, nchunk),
        in_specs=[new(512), new(512), new(512),
                  pl.BlockSpec((1, nblk, 512), lambda b, c, pt: (b, 0, 0)),
                  pl.BlockSpec((rows, 1), lambda b, c, pt: (0, 0))]
        + _page_specs(512, npg, layer) + _page_specs(512, npg, layer),
        out_specs=pl.BlockSpec((t_new, 512), lambda b, c, pt: (b, 0)),
        scratch_shapes=[pltpu.VMEM((rows, 1), F32), pltpu.VMEM((rows, 1), F32), pltpu.VMEM((rows, 512), F32),
                        pltpu.VMEM((rows, nblk), F32)],
    )
    return pl.pallas_call(
        functools.partial(_moba_decode_kernel, npg=npg, t_new=t_new, past=n_pages * PAGE),
        grid_spec=grid_spec,
        out_shape=jax.ShapeDtypeStruct((nreq * t_new, 512), F32),
        compiler_params=_params("parallel", "arbitrary"),
        name="moba_decode",
    )(page_table, q, k_new, v_new, means, slopes, *([pool_k] * npg), *([pool_v] * npg))


def _rot_cols(w):
    half = MLA_ROPE // 2
    return jnp.concatenate([-w[..., half:], w[..., :half]], axis=-1)


def _pad_cols(w, width):
    return jnp.pad(w, [(0, 0)] * (w.ndim - 1) + [(0, width - w.shape[-1])])


def _prep_even(w_in, w_uq, w_uk, w_uv):
    o = MLA_Q_RANK + MLA_KV_RANK
    pe = w_in[:, o:o + MLA_ROPE]
    w_in_x = jnp.concatenate([w_in[:, :o], _pad_cols(pe, 128), _pad_cols(_rot_cols(pe), 128),
                              w_in[:, o + MLA_ROPE:]], axis=1).astype(BF16)
    nope = w_uq[:, :, :MLA_NOPE].reshape(MLA_Q_RANK, MLA_HEADS * MLA_NOPE)
    qpe = w_uq[:, :, MLA_NOPE:]
    w_uq_x = jnp.concatenate([nope, _pad_cols(qpe, 128).reshape(MLA_Q_RANK, -1),
                              _pad_cols(_rot_cols(qpe), 128).reshape(MLA_Q_RANK, -1)], axis=1).astype(BF16)
    uk = jnp.transpose(w_uk, (1, 2, 0))
    uv = jnp.transpose(w_uv, (1, 0, 2))
    z_k = jnp.zeros((MLA_NOPE, MLA_KV_RANK), F32)
    z_v = jnp.zeros((MLA_KV_RANK, MLA_V), F32)
    uk_bd = jnp.stack([jnp.block([[uk[2 * p], z_k], [z_k, uk[2 * p + 1]]]) for p in range(MLA_HEADS // 2)])
    uv_bd = jnp.stack([jnp.block([[uv[2 * p], z_v], [z_v, uv[2 * p + 1]]]) for p in range(MLA_HEADS // 2)])
    return w_in_x, w_uq_x, uk_bd.astype(BF16), uv_bd.astype(BF16)


def _prep_odd(w_in, w_out):
    nq = FOX_HEADS * HEAD_DIM
    nk = FOX_KV_HEADS * HEAD_DIM
    wq = w_in[:, :nq].reshape(D_MODEL, FOX_KV_HEADS, 2, HEAD_DIM).transpose(0, 2, 1, 3).reshape(D_MODEL, nq)
    w_in_x = jnp.concatenate([wq, w_in[:, nq:nq + 2 * nk], _pad_cols(w_in[:, nq + 2 * nk:], 128)],
                             axis=1).astype(BF16)
    w_out_x = w_out.reshape(FOX_KV_HEADS, 2, HEAD_DIM, D_MODEL).transpose(1, 0, 2, 3).reshape(nq, D_MODEL)
    return w_in_x, w_out_x.astype(BF16)


def _rope_tables(pos):
    half = MLA_ROPE // 2
    inv = ROPE_THETA ** (-jnp.arange(half, dtype=F32) / half)
    ang = pos.astype(F32)[:, None] * inv[None, :]
    cos, sin = jnp.cos(ang), jnp.sin(ang)
    return (_pad_cols(jnp.concatenate([cos, cos], axis=-1), 128),
            _pad_cols(jnp.concatenate([sin, sin], axis=-1), 128))


def _row_tile(m):
    for t in (512, 256, 128, 64, 32, 16, 8):
        if m % t == 0:
            return t
    raise ValueError(f"row count {m} is not a multiple of 8")


def kernel(x_prompt, x_sample, cache_mla_ckv, cache_mla_kpe, cache_moba_k, cache_moba_v, cache_fox_k, cache_fox_v, cache_fox_logf, page_table, attn_norm, ffn_norm, final_norm, w_in_even, mla_q_norm, mla_kv_norm, mla_w_uq, mla_w_uk, mla_w_uv, w_out_even, w_in_odd, fox_b_forget, w_out_odd, ffn_w_gate, ffn_w_up, ffn_w_down):
    nb, seq, d = x_prompt.shape
    nreq, t_new, _ = x_sample.shape
    n_pages = page_table.shape[1]
    past = n_pages * PAGE
    depth = attn_norm.shape[0]
    assert d == D_MODEL and seq % MOBA_BLOCK == 0 and past % MOBA_BLOCK == 0 and t_new % 8 == 0
    mp, ms = nb * seq, nreq * t_new
    tmp, tms = _row_tile(mp), _row_tile(ms)

    xp = x_prompt.reshape(mp, d)
    xs = x_sample.reshape(ms, d)
    cs_p, sn_p = _rope_tables(jnp.tile(jnp.arange(seq, dtype=jnp.int32), nb))
    cs_s, sn_s = _rope_tables(jnp.tile(past + jnp.arange(t_new, dtype=jnp.int32), nreq))
    slope_h = jnp.exp2(-8.0 * jnp.arange(1, MOBA_HEADS + 1, dtype=F32) / MOBA_HEADS)
    slopes_p = jnp.repeat(slope_h, Q_BLOCK).reshape(MOBA_HEADS // 2, 2 * Q_BLOCK, 1)
    slopes_s = jnp.repeat(slope_h, t_new).reshape(MOBA_HEADS * t_new, 1)
    pool_moba_k = cache_moba_k.reshape(cache_moba_k.shape[:3] + (512,))
    pool_moba_v = cache_moba_v.reshape(cache_moba_v.shape[:3] + (512,))
    pool_fox_k = cache_fox_k.reshape(cache_fox_k.shape[:3] + (512,))
    pool_fox_v = cache_fox_v.reshape(cache_fox_v.shape[:3] + (512,))
    row1 = lambda v: v.reshape(1, -1)

    outs = {k: [] for k in ("ckv_p", "ckv_s", "kpe_p", "kpe_s", "mk_p", "mk_s", "mv_p", "mv_s",
                            "fk_p", "fk_s", "fv_p", "fv_s", "fl_p", "fl_s")}
    for l in range(depth):
        final = l == depth - 1
        ffn = (row1(ffn_norm[l]), ffn_w_gate[l].astype(BF16), ffn_w_up[l].astype(BF16),
               ffn_w_down[l].astype(BF16), row1(final_norm))
        if l % 2 == 0:
            e = l // 2
            w_in_x, w_uq_x, uk_bd, uv_bd = _prep_even(w_in_even[e], mla_w_uq[e], mla_w_uk[e], mla_w_uv[e])
            w_out = w_out_even[e].astype(BF16)
            common = (row1(attn_norm[l]), w_in_x, row1(mla_q_norm[e]), row1(mla_kv_norm[e]), w_uq_x, uk_bd)
            qcat, kcat, ckv, kpe, mq, mk, mv = _proj_even(xp, *common, cs_p, sn_p, tmp, BF16)
            mix_a = _mla_prompt(qcat, kcat, uv_bd, nb, seq)
            mix_b = _moba_prompt(mq, mk, mv, slopes_p, nb, seq)
            xp = _out_ffn(xp, mix_a, mix_b, w_out, *ffn, tmp, final)
            outs["ckv_p"].append(ckv.reshape(nb, seq, MLA_KV_RANK))
            outs["kpe_p"].append(kpe.reshape(nb, seq, MLA_ROPE))
            outs["mk_p"].append(mk.reshape(nb, seq, MOBA_HEADS, HEAD_DIM))
            outs["mv_p"].append(mv.reshape(nb, seq, MOBA_HEADS, HEAD_DIM))

            qcat, kcat, ckv, kpe, mq, mk, mv = _proj_even(xs, *common, cs_s, sn_s, tms, F32)
            mix_a = _mla_decode(qcat, kcat, uv_bd, cache_mla_ckv, cache_mla_kpe, e, page_table, t_new)
            means = _moba_means(pool_moba_k, e, page_table)
            mix_b = _moba_decode(mq, mk, mv, means, slopes_s, pool_moba_k, pool_moba_v, e, page_table, t_new)
            xs = _out_ffn(xs, mix_a, mix_b, w_out, *ffn, tms, final)
            outs["ckv_s"].append(ckv.reshape(nreq, t_new, MLA_KV_RANK))
            outs["kpe_s"].append(kpe.reshape(nreq, t_new, MLA_ROPE))
            outs["mk_s"].append(mk.reshape(nreq, t_new, MOBA_HEADS, HEAD_DIM))
            outs["mv_s"].append(mv.reshape(nreq, t_new, MOBA_HEADS, HEAD_DIM))
        else:
            o = l // 2
            w_in_x, w_out = _prep_odd(w_in_odd[o], w_out_odd[o])
            common = (row1(attn_norm[l]), w_in_x, row1(fox_b_forget[o]))
            q, k, v, lf = _proj_odd(xp, *common, tmp, BF16)
            f, ft = _fox_cumsum(lf, lf.reshape(nb, seq, FOX_HEADS).transpose(0, 2, 1), nb, seq)
            fq = f.reshape(mp, 4, 4).transpose(1, 0, 2)
            mix = _fox_prompt(q, k, v, fq, ft.reshape(nb, 4, 4, seq), nb, seq)
            xp = _out_ffn(xp, mix[0], mix[1], w_out, *ffn, tmp, final)
            outs["fk_p"].append(k.reshape(nb, seq, FOX_KV_HEADS, HEAD_DIM))
            outs["fv_p"].append(v.reshape(nb, seq, FOX_KV_HEADS, HEAD_DIM))
            outs["fl_p"].append(lf.reshape(nb, seq, FOX_HEADS))

            q, k, v, lf = _proj_odd(xs, *common, tms, F32)
            mix = _fox_decode(q, k, v, lf, pool_fox_k, pool_fox_v, cache_fox_logf, o, page_table, t_new)
            xs = _out_ffn(xs, mix[0], mix[1], w_out, *ffn, tms, final)
            outs["fk_s"].append(k.reshape(nreq, t_new, FOX_KV_HEADS, HEAD_DIM))
            outs["fv_s"].append(v.reshape(nreq, t_new, FOX_KV_HEADS, HEAD_DIM))
            outs["fl_s"].append(lf.reshape(nreq, t_new, FOX_HEADS))

    st = lambda name: jnp.stack(outs[name])
    return (xp.reshape(nb, seq, d), xs.reshape(nreq, t_new, d),
            st("ckv_p"), st("ckv_s"), st("kpe_p"), st("kpe_s"), st("mk_p"), st("mk_s"), st("mv_p"), st("mv_s"),
            st("fk_p"), st("fk_s"), st("fv_p"), st("fv_s"), st("fl_p"), st("fl_s"))
```

```python
import functools

import jax
import jax.numpy as jnp
from jax import lax
from jax.experimental import pallas as pl
from jax.experimental.pallas import tpu as pltpu

F32 = jnp.float32
BF16 = jnp.bfloat16

D_MODEL = 1024
HEAD_DIM = 64
PAGE = 128
RMS_EPS = 1e-6
MLA_HEADS = 8
MLA_Q_RANK = 256
MLA_KV_RANK = 128
MLA_NOPE = 64
MLA_ROPE = 32
MLA_V = 64
ROPE_THETA = 10000.0
MLA_SCALE = (MLA_NOPE + MLA_ROPE) ** -0.5
MOBA_HEADS = 8
MOBA_BLOCK = 256
MOBA_TOPK = 3
MOBA_SCALE = HEAD_DIM ** -0.5
FOX_HEADS = 16
FOX_KV_HEADS = 8
FOX_SCALE = HEAD_DIM ** -0.5
FFN_HIDDEN = 2816
FFN_CHUNK = 1408
LANES = 128
KV_WIDTH = 512
NEG = -1e30
VMEM_LIMIT = 56 * 1024 * 1024
PAGES_PER_STEP = 16
MLA_PAGES_PER_STEP = 32

HI = lax.Precision.HIGHEST


def _nt(a, b, precision=None):
    return lax.dot_general(a, b, (((1,), (1,)), ((), ())), precision=precision,
                           preferred_element_type=F32)


def _dot(a, b, precision=None):
    return jnp.dot(a, b, precision=precision, preferred_element_type=F32)


def _rms(x, g):
    return x * lax.rsqrt(jnp.mean(x * x, axis=-1, keepdims=True) + RMS_EPS) * g


def _params(*sem):
    return pltpu.CompilerParams(dimension_semantics=sem, vmem_limit_bytes=VMEM_LIMIT)


def _const_spec(shape):
    nd = len(shape)
    return pl.BlockSpec(shape, lambda *a: (0,) * nd, pipeline_mode=pl.Buffered(1))


def _iota(shape, dim):
    return lax.broadcasted_iota(jnp.int32, shape, dim)


def _proj_even_kernel(x_ref, g_ref, win_ref, qn_ref, kvn_ref, wuq_ref, wuk_ref, cs_ref, sn_ref,
                      qcat_ref, kcat_ref, ckv_ref, kpe_ref, mq_ref, mk_ref, mv_ref):
    h = _rms(x_ref[...], g_ref[...]).astype(BF16)

    def mm(lo, hi):
        return _dot(h, win_ref[:, lo:hi])

    cq = _rms(mm(0, 256), qn_ref[...]).astype(BF16)
    ckv = _rms(mm(256, 384), kvn_ref[...])
    cs = cs_ref[...]
    sn = sn_ref[...]
    kpe = mm(384, 512) * cs + mm(512, 640) * sn
    ckv_ref[...] = ckv
    kpe_ref[...] = kpe[:, :MLA_ROPE]
    kcat_ref[:, 0:128] = ckv.astype(kcat_ref.dtype)
    kcat_ref[:, 128:256] = kpe.astype(kcat_ref.dtype)
    mq_ref[...] = mm(640, 1152)
    mk_ref[...] = mm(1152, 1664)
    mv_ref[...] = mm(1664, 2176)
    qn = _dot(cq, wuq_ref[:, 0:512]).astype(BF16)
    for p in range(MLA_HEADS // 2):
        ql = _dot(qn[:, 128 * p:128 * p + 128], wuk_ref[p])
        for hh in range(2):
            hd = 2 * p + hh
            qpe = (_dot(cq, wuq_ref[:, 512 + 128 * hd:640 + 128 * hd]) * cs
                   + _dot(cq, wuq_ref[:, 1536 + 128 * hd:1664 + 128 * hd]) * sn)
            qcat_ref[:, 256 * hd:256 * hd + 128] = (ql[:, 128 * hh:128 * hh + 128] * MLA_SCALE).astype(qcat_ref.dtype)
            qcat_ref[:, 256 * hd + 128:256 * hd + 256] = (qpe * MLA_SCALE).astype(qcat_ref.dtype)


def _proj_even(x, g, w_in, qn, kvn, w_uq, w_uk, cs, sn, tm, act_dtype):
    m = x.shape[0]
    row = lambda w: pl.BlockSpec((tm, w), lambda i: (i, 0))
    out_shape = (
        jax.ShapeDtypeStruct((m, 2048), act_dtype),
        jax.ShapeDtypeStruct((m, 256), act_dtype),
        jax.ShapeDtypeStruct((m, 128), F32),
        jax.ShapeDtypeStruct((m, 32), F32),
        jax.ShapeDtypeStruct((m, 512), F32),
        jax.ShapeDtypeStruct((m, 512), F32),
        jax.ShapeDtypeStruct((m, 512), F32),
    )
    return pl.pallas_call(
        _proj_even_kernel,
        grid=(m // tm,),
        in_specs=[row(D_MODEL), _const_spec((1, D_MODEL)), _const_spec(w_in.shape), _const_spec((1, 256)),
                  _const_spec((1, 128)), _const_spec(w_uq.shape), _const_spec(w_uk.shape), row(128), row(128)],
        out_specs=(row(2048), row(256), row(128), row(32), row(512), row(512), row(512)),
        out_shape=out_shape,
        compiler_params=_params("parallel"),
        name="proj_even",
    )(x, g, w_in, qn, kvn, w_uq, w_uk, cs, sn)


def _proj_odd_kernel(x_ref, g_ref, win_ref, bf_ref, q_ref, k_ref, v_ref, lf_ref):
    h = _rms(x_ref[...], g_ref[...]).astype(BF16)

    def mm(lo, hi):
        return _dot(h, win_ref[:, lo:hi])

    q_ref[...] = mm(0, 1024).astype(q_ref.dtype)
    k_ref[...] = mm(1024, 1536)
    v_ref[...] = mm(1536, 2048)
    zf = mm(2048, 2176)[:, :FOX_HEADS] + bf_ref[...]
    lf_ref[...] = -(jnp.maximum(-zf, 0.0) + jnp.log1p(jnp.exp(-jnp.abs(zf))))


def _proj_odd(x, g, w_in, b_f, tm, act_dtype):
    m = x.shape[0]
    row = lambda w: pl.BlockSpec((tm, w), lambda i: (i, 0))
    out_shape = (
        jax.ShapeDtypeStruct((m, 1024), act_dtype),
        jax.ShapeDtypeStruct((m, 512), F32),
        jax.ShapeDtypeStruct((m, 512), F32),
        jax.ShapeDtypeStruct((m, FOX_HEADS), F32),
    )
    return pl.pallas_call(
        _proj_odd_kernel,
        grid=(m // tm,),
        in_specs=[row(D_MODEL), _const_spec((1, D_MODEL)), _const_spec(w_in.shape), _const_spec((1, FOX_HEADS))],
        out_specs=(row(1024), row(512), row(512), row(FOX_HEADS)),
        out_shape=out_shape,
        compiler_params=_params("parallel"),
        name="proj_odd",
    )(x, g, w_in, b_f)


def _out_ffn_kernel(x_ref, a_ref, b_ref, wo_ref, g_ref, wg_ref, wu_ref, wd_ref, fn_ref, o_ref, *, final):
    x1 = (x_ref[...] + _dot(a_ref[...].astype(BF16), wo_ref[0:512, :])
          + _dot(b_ref[...].astype(BF16), wo_ref[512:1024, :]))
    h = _rms(x1, g_ref[...]).astype(BF16)
    acc = jnp.zeros_like(x1)
    for c in range(FFN_HIDDEN // FFN_CHUNK):
        lo = c * FFN_CHUNK
        gate = _dot(h, wg_ref[:, lo:lo + FFN_CHUNK])
        up = _dot(h, wu_ref[:, lo:lo + FFN_CHUNK])
        act = (gate / (1.0 + jnp.exp(-gate))) * up
        acc = acc + _dot(act.astype(BF16), wd_ref[lo:lo + FFN_CHUNK, :])
    out = x1 + acc
    if final:
        out = _rms(out, fn_ref[...])
    o_ref[...] = out


def _out_ffn(x, mix_a, mix_b, w_out, g, w_gate, w_up, w_down, final_norm, tm, final):
    m = x.shape[0]
    row = lambda w: pl.BlockSpec((tm, w), lambda i: (i, 0))
    return pl.pallas_call(
        functools.partial(_out_ffn_kernel, final=final),
        grid=(m // tm,),
        in_specs=[row(D_MODEL), row(512), row(512), _const_spec(w_out.shape), _const_spec((1, D_MODEL)),
                  _const_spec(w_gate.shape), _const_spec(w_up.shape), _const_spec(w_down.shape),
                  _const_spec((1, D_MODEL))],
        out_specs=row(D_MODEL),
        out_shape=jax.ShapeDtypeStruct((m, D_MODEL), F32),
        compiler_params=_params("parallel"),
        name="out_ffn",
    )(x, mix_a, mix_b, w_out, g, w_gate, w_up, w_down, final_norm)


def _init_stats(m_ref, l_ref, acc_ref):
    m_ref[...] = jnp.full(m_ref.shape, NEG, F32)
    l_ref[...] = jnp.zeros(l_ref.shape, F32)
    acc_ref[...] = jnp.zeros(acc_ref.shape, F32)


def _softmax_tile(tiles, rows, m_ref, l_ref, p_ref, acc_ref, alpha_ref=None):
    mx = tiles[0]
    for t in tiles[1:]:
        mx = jnp.maximum(mx, t)
    m_old = m_ref[rows, :]
    m_new = jnp.maximum(m_old, jnp.max(mx, axis=-1, keepdims=True))
    alpha = jnp.exp(m_old - m_new)
    ps = None
    for c, t in enumerate(tiles):
        p = jnp.exp(t - m_new)
        ps = p if ps is None else ps + p
        p_ref[rows, LANES * c:LANES * (c + 1)] = p.astype(p_ref.dtype)
    l_ref[rows, :] = alpha * l_ref[rows, :] + jnp.sum(ps, axis=-1, keepdims=True)
    m_ref[rows, :] = m_new
    if alpha_ref is not None:
        alpha_ref[rows, :] = alpha
        return
    for c in range(acc_ref.shape[1] // LANES):
        acc_ref[rows, LANES * c:LANES * (c + 1)] = acc_ref[rows, LANES * c:LANES * (c + 1)] * alpha


def _causal_blocks(n_plain, qk, softmax, pv):
    qk(0, 0)

    def pair(jj, carry):
        j = 2 * jj
        qk(j + 1, 1)
        softmax(j, 0, False)
        pv(j, 0)
        qk(j + 2, 0)
        softmax(j + 1, 1, False)
        pv(j + 1, 1)
        return carry

    lax.fori_loop(0, n_plain // 2, pair, 0)

    @pl.when(n_plain % 2 == 0)
    def _():
        softmax(n_plain, 0, True)
        pv(n_plain, 0)

    @pl.when(n_plain % 2 == 1)
    def _():
        qk(n_plain, 1)
        softmax(n_plain - 1, 0, False)
        pv(n_plain - 1, 0)
        softmax(n_plain, 1, True)
        pv(n_plain, 1)


def _flash_scratch(rows, tk, width):
    return [pltpu.VMEM((rows, tk), F32), pltpu.VMEM((rows, tk), F32),
            pltpu.VMEM((rows, tk), BF16), pltpu.VMEM((rows, tk), BF16),
            pltpu.VMEM((rows, LANES), F32), pltpu.VMEM((rows, LANES), F32),
            pltpu.VMEM((rows, LANES), F32), pltpu.VMEM((rows, LANES), F32), pltpu.VMEM((rows, width), F32)]


def _mla_prompt_kernel(q_ref, k_ref, wuv_ref, o_ref, qs_ref, s0_ref, s1_ref, p0_ref, p1_ref, a0_ref, a1_ref,
                       m_ref, l_ref, acc_ref, *, tq, tk, rt):
    i = pl.program_id(1)
    nh = MLA_HEADS
    nl = tk // LANES
    s_refs, p_refs, a_refs = (s0_ref, s1_ref), (p0_ref, p1_ref), (a0_ref, a1_ref)
    for h in range(nh):
        qs_ref[h * tq:(h + 1) * tq, :] = q_ref[:, 256 * h:256 * h + 256]
    _init_stats(m_ref, l_ref, acc_ref)
    j_last = (i * tq + tq - 1) // tk

    def qk(j, buf):
        kb = k_ref[pl.ds(pl.multiple_of(j * tk, tk), tk), :]
        s_refs[buf][...] = _nt(qs_ref[...], kb)

    def pv(j, buf):
        vb = k_ref[pl.ds(pl.multiple_of(j * tk, tk), tk), 0:MLA_KV_RANK]
        acc_ref[...] = acc_ref[...] * a_refs[buf][...] + _dot(p_refs[buf][...], vb)

    def softmax(j, buf, masked):
        for t in range(nh * tq // rt):
            rows = slice(t * rt, (t + 1) * rt)
            tiles = []
            for c in range(nl):
                s = s_refs[buf][rows, LANES * c:LANES * (c + 1)]
                if masked:
                    qpos = i * tq + (t * rt) % tq + _iota((rt, 1), 0)
                    kpos = j * tk + LANES * c + _iota((1, LANES), 1)
                    s = jnp.where(kpos <= qpos, s, NEG)
                tiles.append(s)
            _softmax_tile(tiles, rows, m_ref, l_ref, p_refs[buf], acc_ref, a_refs[buf])

    _causal_blocks(j_last, qk, softmax, pv)
    o = acc_ref[...] / l_ref[...]
    for p in range(nh // 2):
        pair = jnp.concatenate([o[(2 * p) * tq:(2 * p + 1) * tq], o[(2 * p + 1) * tq:(2 * p + 2) * tq]], axis=1)
        o_ref[:, 128 * p:128 * p + 128] = _dot(pair.astype(BF16), wuv_ref[p]).astype(o_ref.dtype)


def _mla_prompt(qcat, kcat, w_uv_bd, nb, seq):
    tq, tk, rt = 128, 256, 32
    nq = seq // tq
    rows = MLA_HEADS * tq
    return pl.pallas_call(
        functools.partial(_mla_prompt_kernel, tq=tq, tk=tk, rt=rt),
        grid=(nb, nq),
        in_specs=[pl.BlockSpec((tq, 2048), lambda b, i: (b * nq + i, 0)),
                  pl.BlockSpec((seq, 256), lambda b, i: (b, 0)),
                  _const_spec(w_uv_bd.shape)],
        out_specs=pl.BlockSpec((tq, 512), lambda b, i: (b * nq + i, 0)),
        out_shape=jax.ShapeDtypeStruct((nb * seq, 512), BF16),
        scratch_shapes=[pltpu.VMEM((rows, 256), BF16)] + _flash_scratch(rows, tk, MLA_KV_RANK),
        compiler_params=_params("parallel", "arbitrary"),
        name="mla_prompt",
    )(qcat, kcat, w_uv_bd)


def _top3_mask(scores, n_valid):
    rows, nblk = scores.shape
    lane = _iota((rows, nblk), 1).astype(F32)
    avail = jnp.where(lane < n_valid, 1.0, 0.0)
    sel = jnp.zeros((rows, nblk), F32)
    for _ in range(MOBA_TOPK):
        work = jnp.where(avail > 0.0, scores, -jnp.inf)
        mx = jnp.max(work, axis=-1, keepdims=True)
        cand = jnp.where(work == mx, jnp.where(avail > 0.0, lane, float(nblk)), float(nblk))
        idx = jnp.min(cand, axis=-1, keepdims=True)
        pick = lane == idx
        sel = jnp.where(pick, 1.0, sel)
        avail = jnp.where(pick, 0.0, avail)
    return sel


def _moba_prompt_kernel(q_ref, k_ref, v_ref, slope_ref, o_ref, mean_ref, qbd_ref, sel_ref, ab_ref,
                        s0_ref, s1_ref, p0_ref, p1_ref, a0_ref, a1_ref, m_ref, l_ref, acc_ref, *, tq, nblk, rt):
    i = pl.program_id(2)
    blk = MOBA_BLOCK
    nl = blk // LANES
    rows_all = 2 * tq
    s_refs, p_refs, a_refs = (s0_ref, s1_ref), (p0_ref, p1_ref), (a0_ref, a1_ref)

    @pl.when(i == 0)
    def _():
        for j in range(nblk):
            mean_ref[j:j + 1, :] = jnp.sum(k_ref[j * blk:(j + 1) * blk, :], axis=0, keepdims=True) * (1.0 / blk)

    q = q_ref[...]
    lane = _iota((tq, LANES), 1)
    qbd = jnp.concatenate([jnp.where(lane < HEAD_DIM, q, 0.0), jnp.where(lane >= HEAD_DIM, q, 0.0)], axis=0)
    qbd_ref[...] = (qbd * MOBA_SCALE).astype(BF16)
    sel_ref[...] = _top3_mask(_nt(qbd, mean_ref[...], precision=HI), i)
    qposf = (i * tq + _iota((rows_all, 1), 0) % tq).astype(F32)
    ab_ref[...] = jnp.broadcast_to(-slope_ref[0] * qposf, (rows_all, LANES))
    _init_stats(m_ref, l_ref, acc_ref)

    def qk(j, buf):
        kb = k_ref[pl.ds(pl.multiple_of(j * blk, blk), blk), :].astype(BF16)
        s_refs[buf][...] = _nt(qbd_ref[...], kb)

    def pv(j, buf):
        vb = v_ref[pl.ds(pl.multiple_of(j * blk, blk), blk), :].astype(BF16)
        acc_ref[...] = acc_ref[...] * a_refs[buf][...] + _dot(p_refs[buf][...], vb)

    def softmax(j, buf, own):
        for g in range(2):
            sg = slope_ref[0, g * tq:g * tq + 1, :]
            cols = [sg * (j * blk + LANES * c + _iota((1, LANES), 1)).astype(F32) for c in range(nl)]
            for t in range(tq // rt):
                r0 = g * tq + t * rt
                rows = slice(r0, r0 + rt)
                rb = ab_ref[rows, :]
                if not own:
                    chosen = jnp.sum(jnp.where(_iota((rt, nblk), 1) == j, sel_ref[rows, :], 0.0),
                                     axis=-1, keepdims=True)
                    rb = rb + jnp.where(chosen > 0.0, 0.0, NEG)
                tiles = []
                for c in range(nl):
                    s = s_refs[buf][rows, LANES * c:LANES * (c + 1)] + (rb + cols[c])
                    if own:
                        s = jnp.where(LANES * c + _iota((1, LANES), 1) <= t * rt + _iota((rt, 1), 0), s, NEG)
                    tiles.append(s)
                _softmax_tile(tiles, rows, m_ref, l_ref, p_refs[buf], acc_ref, a_refs[buf])

    qk(i, 0)
    qk(0, 1)
    softmax(i, 0, True)
    pv(i, 0)

    def pair(jj, carry):
        j = 2 * jj
        qk(j + 1, 0)
        softmax(j, 1, False)
        pv(j, 1)
        qk(j + 2, 1)
        softmax(j + 1, 0, False)
        pv(j + 1, 0)
        return carry

    lax.fori_loop(0, i // 2, pair, 0)

    @pl.when(i % 2 == 1)
    def _():
        softmax(i - 1, 1, False)
        pv(i - 1, 1)

    o = acc_ref[...] / l_ref[...]
    o_ref[...] = jnp.where(lane < HEAD_DIM, o[0:tq], o[tq:2 * tq]).astype(o_ref.dtype)


def _moba_prompt(mq, mk, mv, slopes, nb, seq):
    tq, rt = MOBA_BLOCK, 32
    nq = seq // tq
    nblk = seq // MOBA_BLOCK
    npair = MOBA_HEADS // 2
    rows = 2 * tq
    return pl.pallas_call(
        functools.partial(_moba_prompt_kernel, tq=tq, nblk=nblk, rt=rt),
        grid=(nb, npair, nq),
        in_specs=[pl.BlockSpec((tq, LANES), lambda b, p, i: (b * nq + i, p)),
                  pl.BlockSpec((seq, LANES), lambda b, p, i: (b, p)),
                  pl.BlockSpec((seq, LANES), lambda b, p, i: (b, p)),
                  pl.BlockSpec((1, rows, 1), lambda b, p, i: (p, 0, 0))],
        out_specs=pl.BlockSpec((tq, LANES), lambda b, p, i: (b * nq + i, p)),
        out_shape=jax.ShapeDtypeStruct((nb * seq, 512), BF16),
        scratch_shapes=[pltpu.VMEM((nblk, LANES), F32), pltpu.VMEM((rows, LANES), BF16),
                        pltpu.VMEM((rows, nblk), F32), pltpu.VMEM((rows, LANES), F32)]
        + _flash_scratch(rows, MOBA_BLOCK, LANES),
        compiler_params=_params("parallel", "parallel", "arbitrary"),
        name="moba_prompt",
    )(mq, mk, mv, slopes)


def _fox_cumsum_kernel(lf_ref, lft_ref, f_ref, ft_ref, *, seq, ch):
    r = _iota((ch, ch), 0)
    c = _iota((ch, ch), 1)
    lower = (c <= r).astype(F32)
    upper = (r <= c).astype(F32)
    carry = jnp.zeros((1, FOX_HEADS), F32)
    carry_t = jnp.zeros((FOX_HEADS, 1), F32)
    for j in range(seq // ch):
        f = _dot(lower, lf_ref[j * ch:(j + 1) * ch, :], precision=HI) + carry
        f_ref[j * ch:(j + 1) * ch, :] = f
        carry = f[ch - 1:ch, :]
        ft = _dot(lft_ref[0, :, j * ch:(j + 1) * ch], upper, precision=HI) + carry_t
        ft_ref[0, :, j * ch:(j + 1) * ch] = ft
        carry_t = ft[:, ch - 1:ch]


def _fox_cumsum(logf, logf_t, nb, seq):
    return pl.pallas_call(
        functools.partial(_fox_cumsum_kernel, seq=seq, ch=256),
        grid=(nb,),
        in_specs=[pl.BlockSpec((seq, FOX_HEADS), lambda b: (b, 0)),
                  pl.BlockSpec((1, FOX_HEADS, seq), lambda b: (b, 0, 0))],
        out_specs=(pl.BlockSpec((seq, FOX_HEADS), lambda b: (b, 0)),
                   pl.BlockSpec((1, FOX_HEADS, seq), lambda b: (b, 0, 0))),
        out_shape=(jax.ShapeDtypeStruct((nb * seq, FOX_HEADS), F32),
                   jax.ShapeDtypeStruct((nb, FOX_HEADS, seq), F32)),
        compiler_params=_params("parallel"),
        name="fox_cumsum",
    )(logf, logf_t)


def _fox_prompt_kernel(q0_ref, q1_ref, k_ref, v_ref, fq_ref, fk_ref, o_ref, qbd_ref, fqr_ref,
                       s0_ref, s1_ref, p0_ref, p1_ref, a0_ref, a1_ref, m_ref, l_ref, acc_ref, *, tq, rt):
    i = pl.program_id(2)
    tk = tq
    nl = tk // LANES
    s_refs, p_refs, a_refs = (s0_ref, s1_ref), (p0_ref, p1_ref), (a0_ref, a1_ref)
    lane = _iota((tq, LANES), 1)
    lo = lane < HEAD_DIM
    q0 = q0_ref[...] * FOX_SCALE
    q1 = q1_ref[...] * FOX_SCALE
    zero = jnp.zeros_like(q0)
    for r, blkq in enumerate((jnp.where(lo, q0, zero), jnp.where(lo, q1, zero),
                              jnp.where(lo, zero, q0), jnp.where(lo, zero, q1))):
        qbd_ref[r * tq:(r + 1) * tq, :] = blkq
        fqr_ref[r * tq:(r + 1) * tq, :] = jnp.broadcast_to(fq_ref[0, :, r:r + 1], (tq, LANES))
    _init_stats(m_ref, l_ref, acc_ref)

    def qk(j, buf):
        kb = k_ref[pl.ds(pl.multiple_of(j * tk, tk), tk), :].astype(BF16)
        s_refs[buf][...] = _nt(qbd_ref[...], kb)

    def pv(j, buf):
        vb = v_ref[pl.ds(pl.multiple_of(j * tk, tk), tk), :].astype(BF16)
        acc_ref[...] = acc_ref[...] * a_refs[buf][...] + _dot(p_refs[buf][...], vb)

    def softmax(j, buf, masked):
        for r in range(4):
            fk = [fk_ref[0, 0, r:r + 1, pl.ds(pl.multiple_of(j * tk + LANES * c, LANES), LANES)] for c in range(nl)]
            for t in range(tq // rt):
                r0 = r * tq + t * rt
                rows = slice(r0, r0 + rt)
                fq = fqr_ref[rows, :]
                tiles = []
                for c in range(nl):
                    s = s_refs[buf][rows, LANES * c:LANES * (c + 1)] + (fq - fk[c])
                    if masked:
                        s = jnp.where(LANES * c + _iota((1, LANES), 1) <= t * rt + _iota((rt, 1), 0), s, NEG)
                    tiles.append(s)
                _softmax_tile(tiles, rows, m_ref, l_ref, p_refs[buf], acc_ref, a_refs[buf])

    _causal_blocks(i, qk, softmax, pv)
    o = acc_ref[...] / l_ref[...]
    o_ref[0] = jnp.where(lo, o[0:tq], o[2 * tq:3 * tq]).astype(o_ref.dtype)
    o_ref[1] = jnp.where(lo, o[tq:2 * tq], o[3 * tq:4 * tq]).astype(o_ref.dtype)


def _fox_prompt(q, k, v, fq, fk, nb, seq):
    tq, rt = 256, 32
    nq = seq // tq
    npair = FOX_KV_HEADS // 2
    rows = 4 * tq
    return pl.pallas_call(
        functools.partial(_fox_prompt_kernel, tq=tq, rt=rt),
        grid=(nb, npair, nq),
        in_specs=[pl.BlockSpec((tq, LANES), lambda b, p, i: (b * nq + i, p)),
                  pl.BlockSpec((tq, LANES), lambda b, p, i: (b * nq + i, npair + p)),
                  pl.BlockSpec((seq, LANES), lambda b, p, i: (b, p)),
                  pl.BlockSpec((seq, LANES), lambda b, p, i: (b, p)),
                  pl.BlockSpec((1, tq, 4), lambda b, p, i: (p, b * nq + i, 0)),
                  pl.BlockSpec((1, 1, 4, seq), lambda b, p, i: (b, p, 0, 0))],
        out_specs=pl.BlockSpec((2, tq, LANES), lambda b, p, i: (0, b * nq + i, p)),
        out_shape=jax.ShapeDtypeStruct((2, nb * seq, 512), BF16),
        scratch_shapes=[pltpu.VMEM((rows, LANES), BF16), pltpu.VMEM((rows, LANES), F32)]
        + _flash_scratch(rows, tq, LANES),
        compiler_params=_params("parallel", "parallel", "arbitrary"),
        name="fox_prompt",
    )(q, q, k, v, fq, fk)


def _page_specs(block, npg, layer, reverse_chunks=None):
    specs = []
    for pg in range(npg):
        if reverse_chunks is None:
            imap = lambda b, c, pt, pg=pg: (layer, pt[b, c * npg + pg], 0, 0)
        else:
            imap = lambda b, c, pt, pg=pg: (layer, pt[b, (reverse_chunks - 1 - c) * npg + pg], 0, 0)
        specs.append(pl.BlockSpec((1, 1) + block, imap))
    return specs


def _new_token_init(qbd16, kn, vn, bias_new, m_ref, l_ref, acc_ref, t_new):
    rows = qbd16.shape[0]
    s = _nt(qbd16, kn.astype(BF16))
    if bias_new is not None:
        s = s + bias_new
    tq = _iota((rows, t_new), 0) % t_new
    tk = _iota((rows, t_new), 1)
    s = jnp.where(tk <= tq, s, NEG)
    m = jnp.max(s, axis=-1, keepdims=True)
    p = jnp.exp(s - m)
    acc = jnp.zeros((rows, vn.shape[1]), F32)
    for t in range(t_new):
        acc = acc + p[:, t:t + 1] * vn[t:t + 1, :]
    m_ref[...] = jnp.broadcast_to(m, m_ref.shape)
    l_ref[...] = jnp.broadcast_to(jnp.sum(p, axis=-1, keepdims=True), l_ref.shape)
    acc_ref[...] = acc


def _mla_decode_kernel(pt_ref, q_ref, kn_ref, wuv_ref, *rest, npg, t_new, rt):
    ckv_refs = rest[:npg]
    kpe_refs = rest[npg:2 * npg]
    o_ref, qs_ref, ck_ref, kpt_ref, s_ref, p_ref, m_ref, l_ref, acc_ref = rest[2 * npg:]
    c = pl.program_id(1)
    nh = MLA_HEADS
    rows_all = nh * t_new
    tk = npg * PAGE

    @pl.when(c == 0)
    def _():
        q = q_ref[...]
        qs = jnp.concatenate([q[:, 256 * h:256 * h + 256] for h in range(nh)], axis=0).astype(BF16)
        qs_ref[...] = qs
        kn = kn_ref[...]
        _new_token_init(qs, kn, kn[:, 0:MLA_KV_RANK], None, m_ref, l_ref, acc_ref, t_new)

    for pg in range(npg):
        ck_ref[PAGE * pg:PAGE * (pg + 1), :] = ckv_refs[pg][0, 0].astype(BF16)
        kpt_ref[:, PAGE * pg:PAGE * (pg + 1)] = kpe_refs[pg][0, 0].astype(BF16)
    s_ref[...] = (_nt(qs_ref[:, 0:MLA_KV_RANK], ck_ref[...])
                  + _dot(qs_ref[:, MLA_KV_RANK:MLA_KV_RANK + MLA_ROPE], kpt_ref[...]))
    for t in range(rows_all // rt):
        rows = slice(t * rt, (t + 1) * rt)
        tiles = [s_ref[rows, LANES * cl:LANES * (cl + 1)] for cl in range(tk // LANES)]
        _softmax_tile(tiles, rows, m_ref, l_ref, p_ref, acc_ref)
    acc_ref[...] += _dot(p_ref[...], ck_ref[...])

    @pl.when(c == pl.num_programs(1) - 1)
    def _():
        o = acc_ref[...] / l_ref[...]
        for p in range(nh // 2):
            pair = jnp.concatenate([o[(2 * p) * t_new:(2 * p + 1) * t_new],
                                    o[(2 * p + 1) * t_new:(2 * p + 2) * t_new]], axis=1)
            o_ref[:, 128 * p:128 * p + 128] = _dot(pair.astype(BF16), wuv_ref[p]).astype(o_ref.dtype)


def _mla_decode(qcat, kcat, w_uv_bd, pool_ckv, pool_kpe_t, layer, page_table, t_new):
    nreq, n_pages = page_table.shape
    npg = min(MLA_PAGES_PER_STEP, n_pages)
    nchunk = n_pages // npg
    rows = MLA_HEADS * t_new
    tk = npg * PAGE
    grid_spec = pltpu.PrefetchScalarGridSpec(
        num_scalar_prefetch=1,
        grid=(nreq, nchunk),
        in_specs=[pl.BlockSpec((t_new, 2048), lambda b, c, pt: (b, 0)),
                  pl.BlockSpec((t_new, 256), lambda b, c, pt: (b, 0)),
                  pl.BlockSpec(w_uv_bd.shape, lambda b, c, pt: (0, 0, 0))]
        + _page_specs((PAGE, MLA_KV_RANK), npg, layer) + _page_specs((MLA_ROPE, PAGE), npg, layer),
        out_specs=pl.BlockSpec((t_new, 512), lambda b, c, pt: (b, 0)),
        scratch_shapes=[pltpu.VMEM((rows, 256), BF16), pltpu.VMEM((tk, MLA_KV_RANK), BF16),
                        pltpu.VMEM((MLA_ROPE, tk), BF16), pltpu.VMEM((rows, tk), F32), pltpu.VMEM((rows, tk), BF16),
                        pltpu.VMEM((rows, LANES), F32), pltpu.VMEM((rows, LANES), F32),
                        pltpu.VMEM((rows, MLA_KV_RANK), F32)],
    )
    return pl.pallas_call(
        functools.partial(_mla_decode_kernel, npg=npg, t_new=t_new, rt=16),
        grid_spec=grid_spec,
        out_shape=jax.ShapeDtypeStruct((nreq * t_new, 512), F32),
        compiler_params=_params("parallel", "arbitrary"),
        name="mla_decode",
    )(page_table, qcat, kcat, w_uv_bd, *([pool_ckv] * npg), *([pool_kpe_t] * npg))


def _moba_means_kernel(pt_ref, *rest, npg):
    k_refs = rest[:npg]
    o_ref = rest[npg]
    c = pl.program_id(1)
    ppb = MOBA_BLOCK // PAGE
    bpc = npg // ppb
    nblk = o_ref.shape[2]

    @pl.when(c == 0)
    def _():
        o_ref[...] = jnp.zeros(o_ref.shape, F32)

    lane = _iota((KV_WIDTH, nblk), 1)
    acc = o_ref[0]
    for j in range(bpc):
        tot = k_refs[ppb * j][0, 0]
        for pg in range(1, ppb):
            tot = tot + k_refs[ppb * j + pg][0, 0]
        col = jnp.sum(tot, axis=1, keepdims=True) * (1.0 / MOBA_BLOCK)
        acc = jnp.where(lane == c * bpc + j, col, acc)
    o_ref[0] = acc


def _moba_means(pool_k_t, layer, page_table):
    nreq, n_pages = page_table.shape
    npg = min(PAGES_PER_STEP, n_pages)
    nchunk = n_pages // npg
    nblk = n_pages * PAGE // MOBA_BLOCK
    grid_spec = pltpu.PrefetchScalarGridSpec(
        num_scalar_prefetch=1,
        grid=(nreq, nchunk),
        in_specs=_page_specs((KV_WIDTH, PAGE), npg, layer),
        out_specs=pl.BlockSpec((1, KV_WIDTH, nblk), lambda b, c, pt: (b, 0, 0)),
    )
    return pl.pallas_call(
        functools.partial(_moba_means_kernel, npg=npg),
        grid_spec=grid_spec,
        out_shape=jax.ShapeDtypeStruct((nreq, KV_WIDTH, nblk), F32),
        compiler_params=_params("parallel", "arbitrary"),
        name="moba_means",
    )(page_table, *([pool_k_t] * npg))


def _head_mask(rows_per_group, n_groups, head_of_group):
    rows = n_groups * rows_per_group
    grp = _iota((rows, KV_WIDTH), 0) // rows_per_group
    blk = _iota((rows, KV_WIDTH), 1) // HEAD_DIM
    return blk == head_of_group(grp)


def _stage_pages(k_refs, v_refs, kt_ref, vt_ref):
    for pg in range(len(k_refs)):
        kt_ref[:, PAGE * pg:PAGE * (pg + 1)] = k_refs[pg][0, 0].astype(BF16)
        vt_ref[:, PAGE * pg:PAGE * (pg + 1)] = v_refs[pg][0, 0].astype(BF16)


def _fox_decode_kernel(pt_ref, q_ref, kn_ref, vn_ref, lfn_ref, *rest, npg, t_new, rt):
    k_refs = rest[:npg]
    v_refs = rest[npg:2 * npg]
    lf_refs = rest[2 * npg:3 * npg]
    (o_ref, qbd_ref, kt_ref, vt_ref, sfx_ref, s_ref, p_ref, m_ref, l_ref, acc_ref,
     carry_ref, gq_ref) = rest[3 * npg:]
    c = pl.program_id(1)
    nhq = FOX_HEADS
    rows_all = nhq * t_new
    tk = npg * PAGE
    hmask = _head_mask(t_new, nhq, lambda grp: grp // 2)

    @pl.when(c == 0)
    def _():
        q = q_ref[...] * FOX_SCALE
        qrep = jnp.concatenate([q[:, KV_WIDTH * (hq % 2):KV_WIDTH * (hq % 2 + 1)] for hq in range(nhq)], axis=0)
        qbd16 = jnp.where(hmask, qrep, 0.0).astype(BF16)
        qbd_ref[...] = qbd16
        eye = (_iota((nhq, nhq), 0) == _iota((nhq, nhq), 1)).astype(F32)
        lfn = lfn_ref[...]
        tri = (_iota((t_new, t_new), 1) <= _iota((t_new, t_new), 0)).astype(F32)
        gct = _nt(eye, _dot(tri, lfn, precision=HI), precision=HI)
        g = jnp.concatenate([jnp.broadcast_to(gct[hq:hq + 1, :], (t_new, t_new)) for hq in range(nhq)], axis=0)
        tq = _iota((rows_all, t_new), 0) % t_new
        tl = _iota((rows_all, t_new), 1)
        gq = jnp.sum(jnp.where(tl == tq, g, 0.0), axis=-1, keepdims=True)
        gq_ref[...] = jnp.broadcast_to(gq, gq_ref.shape)
        carry_ref[...] = jnp.zeros(carry_ref.shape, F32)
        _new_token_init(qbd16, kn_ref[...], vn_ref[...], gq - g, m_ref, l_ref, acc_ref, t_new)

    _stage_pages(k_refs, v_refs, kt_ref, vt_ref)
    lft = jnp.concatenate([r[0, 0] for r in lf_refs], axis=1)
    lane = _iota((nhq, tk), 1)
    incl = lft
    sh = 1
    while sh < tk:
        incl = incl + jnp.where(lane < tk - sh, pltpu.roll(incl, tk - sh, axis=1), 0.0)
        sh *= 2
    sfx_ref[...] = incl - lft + carry_ref[...]
    carry_ref[...] = carry_ref[...] + incl[:, 0:1]
    s_ref[...] = _dot(qbd_ref[...], kt_ref[...])
    hq_per_tile = rt // t_new
    for t in range(rows_all // rt):
        rows = slice(t * rt, (t + 1) * rt)
        gq = gq_ref[rows, :]
        tiles = []
        for cl in range(tk // LANES):
            lanes = slice(LANES * cl, LANES * (cl + 1))
            sfx = jnp.concatenate([jnp.broadcast_to(sfx_ref[hq:hq + 1, lanes], (t_new, LANES))
                                   for hq in range(t * hq_per_tile, (t + 1) * hq_per_tile)], axis=0)
            tiles.append(s_ref[rows, lanes] + (sfx + gq))
        _softmax_tile(tiles, rows, m_ref, l_ref, p_ref, acc_ref)
    acc_ref[...] += _nt(p_ref[...], vt_ref[...])

    @pl.when(c == pl.num_programs(1) - 1)
    def _():
        o = jnp.where(hmask, acc_ref[...] / l_ref[:, 0:1], 0.0)
        for g in range(2):
            tot = jnp.zeros((t_new, KV_WIDTH), F32)
            for kvh in range(FOX_KV_HEADS):
                r0 = (kvh * 2 + g) * t_new
                tot = tot + o[r0:r0 + t_new, :]
            o_ref[g] = tot


def _fox_decode(q, k_new, v_new, lf_new, pool_k_t, pool_v_t, pool_lf_t, layer, page_table, t_new):
    nreq, n_pages = page_table.shape
    npg = min(PAGES_PER_STEP, n_pages)
    nchunk = n_pages // npg
    rows = FOX_HEADS * t_new
    tk = npg * PAGE
    new = lambda w: pl.BlockSpec((t_new, w), lambda b, c, pt: (b, 0))
    grid_spec = pltpu.PrefetchScalarGridSpec(
        num_scalar_prefetch=1,
        grid=(nreq, nchunk),
        in_specs=[new(1024), new(KV_WIDTH), new(KV_WIDTH), new(FOX_HEADS)]
        + _page_specs((KV_WIDTH, PAGE), npg, layer, nchunk) + _page_specs((KV_WIDTH, PAGE), npg, layer, nchunk)
        + _page_specs((FOX_HEADS, PAGE), npg, layer, nchunk),
        out_specs=pl.BlockSpec((2, t_new, KV_WIDTH), lambda b, c, pt: (0, b, 0)),
        scratch_shapes=[pltpu.VMEM((rows, KV_WIDTH), BF16), pltpu.VMEM((KV_WIDTH, tk), BF16),
                        pltpu.VMEM((KV_WIDTH, tk), BF16), pltpu.VMEM((FOX_HEADS, tk), F32),
                        pltpu.VMEM((rows, tk), F32), pltpu.VMEM((rows, tk), BF16),
                        pltpu.VMEM((rows, LANES), F32), pltpu.VMEM((rows, LANES), F32),
                        pltpu.VMEM((rows, KV_WIDTH), F32), pltpu.VMEM((FOX_HEADS, 1), F32),
                        pltpu.VMEM((rows, LANES), F32)],
    )
    return pl.pallas_call(
        functools.partial(_fox_decode_kernel, npg=npg, t_new=t_new, rt=16),
        grid_spec=grid_spec,
        out_shape=jax.ShapeDtypeStruct((2, nreq * t_new, KV_WIDTH), F32),
        compiler_params=_params("parallel", "arbitrary"),
        name="fox_decode",
    )(page_table, q, k_new, v_new, lf_new, *([pool_k_t] * npg), *([pool_v_t] * npg), *([pool_lf_t] * npg))


def _moba_decode_kernel(pt_ref, q_ref, kn_ref, vn_ref, mean_ref, slope_ref, *rest, npg, t_new, past, rt):
    k_refs = rest[:npg]
    v_refs = rest[npg:2 * npg]
    o_ref, qbd_ref, kt_ref, vt_ref, sel_ref, ab_ref, sl_ref, s_ref, p_ref, m_ref, l_ref, acc_ref = rest[2 * npg:]
    c = pl.program_id(1)
    nh = MOBA_HEADS
    rows_all = nh * t_new
    tk = npg * PAGE
    bpc = tk // MOBA_BLOCK
    lpb = MOBA_BLOCK // LANES
    nblk = mean_ref.shape[2]
    hmask = _head_mask(t_new, nh, lambda grp: grp)

    @pl.when(c == 0)
    def _():
        q = q_ref[...]
        qbd = jnp.where(hmask, jnp.concatenate([q] * nh, axis=0), 0.0)
        qbd16 = (qbd * MOBA_SCALE).astype(BF16)
        qbd_ref[...] = qbd16
        sel_ref[...] = _top3_mask(_dot(qbd, mean_ref[0], precision=HI), nblk)
        slope = slope_ref[...]
        t_of_row = _iota((rows_all, 1), 0) % t_new
        ab_ref[...] = jnp.broadcast_to(-slope * (past + t_of_row).astype(F32), ab_ref.shape)
        sl_ref[...] = jnp.broadcast_to(slope, sl_ref.shape)
        bias_new = -slope * (t_of_row - _iota((1, t_new), 1)).astype(F32)
        _new_token_init(qbd16, kn_ref[...], vn_ref[...], bias_new, m_ref, l_ref, acc_ref, t_new)

    _stage_pages(k_refs, v_refs, kt_ref, vt_ref)
    s_ref[...] = _dot(qbd_ref[...], kt_ref[...])
    for t in range(rows_all // rt):
        rows = slice(t * rt, (t + 1) * rt)
        ab = ab_ref[rows, :]
        sl = sl_ref[rows, :]
        sel = sel_ref[rows, :]
        tiles = []
        for j in range(bpc):
            chosen = jnp.sum(jnp.where(_iota((rt, nblk), 1) == c * bpc + j, sel, 0.0), axis=-1, keepdims=True)
            keep = jnp.where(chosen > 0.0, 0.0, NEG)
            for cl in range(lpb * j, lpb * (j + 1)):
                lanes = slice(LANES * cl, LANES * (cl + 1))
                kposf = (c * tk + LANES * cl + _iota((1, LANES), 1)).astype(F32)
                tiles.append(s_ref[rows, lanes] + ((ab + sl * kposf) + keep))
        _softmax_tile(tiles, rows, m_ref, l_ref, p_ref, acc_ref)
    acc_ref[...] += _nt(p_ref[...], vt_ref[...])

    @pl.when(c == pl.num_programs(1) - 1)
    def _():
        o = jnp.where(hmask, acc_ref[...] / l_ref[:, 0:1], 0.0)
        tot = jnp.zeros((t_new, KV_WIDTH), F32)
        for h in range(nh):
            tot = tot + o[h * t_new:(h + 1) * t_new, :]
        o_ref[...] = tot


def _moba_decode(q, k_new, v_new, means_t, slopes, pool_k_t, pool_v_t, layer, page_table, t_new):
    nreq, n_pages = page_table.shape
    npg = min(PAGES_PER_STEP, n_pages)
    nchunk = n_pages // npg
    rows = MOBA_HEADS * t_new
    tk = npg * PAGE
    nblk = means_t.shape[2]
    new = lambda w: pl.BlockSpec((t_new, w), lambda b, c, pt: (b, 0))
    grid_spec = pltpu.PrefetchScalarGridSpec(
        num_scalar_prefetch=1,
        grid=(nreq, nchunk),
        in_specs=[new(KV_WIDTH), new(KV_WIDTH), new(KV_WIDTH),
                  pl.BlockSpec((1, KV_WIDTH, nblk), lambda b, c, pt: (b, 0, 0)),
                  pl.BlockSpec((rows, 1), lambda b, c, pt: (0, 0))]
        + _page_specs((KV_WIDTH, PAGE), npg, layer) + _page_specs((KV_WIDTH, PAGE), npg, layer),
        out_specs=pl.BlockSpec((t_new, KV_WIDTH), lambda b, c, pt: (b, 0)),
        scratch_shapes=[pltpu.VMEM((rows, KV_WIDTH), BF16), pltpu.VMEM((KV_WIDTH, tk), BF16),
                        pltpu.VMEM((KV_WIDTH, tk), BF16), pltpu.VMEM((rows, nblk), F32),
                        pltpu.VMEM((rows, LANES), F32), pltpu.VMEM((rows, LANES), F32),
                        pltpu.VMEM((rows, tk), F32), pltpu.VMEM((rows, tk), BF16),
                        pltpu.VMEM((rows, LANES), F32), pltpu.VMEM((rows, LANES), F32),
                        pltpu.VMEM((rows, KV_WIDTH), F32)],
    )
    return pl.pallas_call(
        functools.partial(_moba_decode_kernel, npg=npg, t_new=t_new, past=n_pages * PAGE, rt=16),
        grid_spec=grid_spec,
        out_shape=jax.ShapeDtypeStruct((nreq * t_new, KV_WIDTH), F32),
        compiler_params=_params("parallel", "arbitrary"),
        name="moba_decode",
    )(page_table, q, k_new, v_new, means_t, slopes, *([pool_k_t] * npg), *([pool_v_t] * npg))


def _rot_cols(w):
    half = MLA_ROPE // 2
    return jnp.concatenate([-w[..., half:], w[..., :half]], axis=-1)


def _pad_cols(w, width):
    return jnp.pad(w, [(0, 0)] * (w.ndim - 1) + [(0, width - w.shape[-1])])


def _prep_even(w_in, w_uq, w_uk, w_uv):
    o = MLA_Q_RANK + MLA_KV_RANK
    pe = w_in[:, o:o + MLA_ROPE]
    w_in_x = jnp.concatenate([w_in[:, :o], _pad_cols(pe, 128), _pad_cols(_rot_cols(pe), 128),
                              w_in[:, o + MLA_ROPE:]], axis=1).astype(BF16)
    nope = w_uq[:, :, :MLA_NOPE].reshape(MLA_Q_RANK, MLA_HEADS * MLA_NOPE)
    qpe = w_uq[:, :, MLA_NOPE:]
    w_uq_x = jnp.concatenate([nope, _pad_cols(qpe, 128).reshape(MLA_Q_RANK, -1),
                              _pad_cols(_rot_cols(qpe), 128).reshape(MLA_Q_RANK, -1)], axis=1).astype(BF16)
    uk = jnp.transpose(w_uk, (1, 2, 0))
    uv = jnp.transpose(w_uv, (1, 0, 2))
    z_k = jnp.zeros((MLA_NOPE, MLA_KV_RANK), F32)
    z_v = jnp.zeros((MLA_KV_RANK, MLA_V), F32)
    uk_bd = jnp.stack([jnp.block([[uk[2 * p], z_k], [z_k, uk[2 * p + 1]]]) for p in range(MLA_HEADS // 2)])
    uv_bd = jnp.stack([jnp.block([[uv[2 * p], z_v], [z_v, uv[2 * p + 1]]]) for p in range(MLA_HEADS // 2)])
    return w_in_x, w_uq_x, uk_bd.astype(BF16), uv_bd.astype(BF16)


def _prep_odd(w_in, w_out):
    nq = FOX_HEADS * HEAD_DIM
    nk = FOX_KV_HEADS * HEAD_DIM
    wq = w_in[:, :nq].reshape(D_MODEL, FOX_KV_HEADS, 2, HEAD_DIM).transpose(0, 2, 1, 3).reshape(D_MODEL, nq)
    w_in_x = jnp.concatenate([wq, w_in[:, nq:nq + 2 * nk], _pad_cols(w_in[:, nq + 2 * nk:], 128)],
                             axis=1).astype(BF16)
    w_out_x = w_out.reshape(FOX_KV_HEADS, 2, HEAD_DIM, D_MODEL).transpose(1, 0, 2, 3).reshape(nq, D_MODEL)
    return w_in_x, w_out_x.astype(BF16)


def _rope_tables(pos):
    half = MLA_ROPE // 2
    inv = ROPE_THETA ** (-jnp.arange(half, dtype=F32) / half)
    ang = pos.astype(F32)[:, None] * inv[None, :]
    cos, sin = jnp.cos(ang), jnp.sin(ang)
    return (_pad_cols(jnp.concatenate([cos, cos], axis=-1), 128),
            _pad_cols(jnp.concatenate([sin, sin], axis=-1), 128))


def _row_tile(m):
    for t in (512, 256, 128, 64, 32, 16, 8):
        if m % t == 0:
            return t
    raise ValueError(f"row count {m} is not a multiple of 8")


def _kv_pool_t(pool):
    l, n, pg, h, d = pool.shape
    return jnp.transpose(pool, (0, 1, 3, 4, 2)).reshape(l, n, h * d, pg)


def kernel(x_prompt, x_sample, cache_mla_ckv, cache_mla_kpe, cache_moba_k, cache_moba_v, cache_fox_k, cache_fox_v, cache_fox_logf, page_table, attn_norm, ffn_norm, final_norm, w_in_even, mla_q_norm, mla_kv_norm, mla_w_uq, mla_w_uk, mla_w_uv, w_out_even, w_in_odd, fox_b_forget, w_out_odd, ffn_w_gate, ffn_w_up, ffn_w_down):
    nb, seq, d = x_prompt.shape
    nreq, t_new, _ = x_sample.shape
    n_pages = page_table.shape[1]
    past = n_pages * PAGE
    depth = attn_norm.shape[0]
    assert d == D_MODEL and seq % MOBA_BLOCK == 0 and past % MOBA_BLOCK == 0 and t_new == 8
    assert n_pages % min(PAGES_PER_STEP, n_pages) == 0 and n_pages % min(MLA_PAGES_PER_STEP, n_pages) == 0
    mp, ms = nb * seq, nreq * t_new
    tmp, tms = _row_tile(mp), _row_tile(ms)

    xp = x_prompt.reshape(mp, d)
    xs = x_sample.reshape(ms, d)
    cs_p, sn_p = _rope_tables(jnp.tile(jnp.arange(seq, dtype=jnp.int32), nb))
    cs_s, sn_s = _rope_tables(jnp.tile(past + jnp.arange(t_new, dtype=jnp.int32), nreq))
    slope_h = jnp.exp2(-8.0 * jnp.arange(1, MOBA_HEADS + 1, dtype=F32) / MOBA_HEADS)
    slopes_p = jnp.repeat(slope_h, MOBA_BLOCK).reshape(MOBA_HEADS // 2, 2 * MOBA_BLOCK, 1)
    slopes_s = jnp.repeat(slope_h, t_new).reshape(MOBA_HEADS * t_new, 1)
    pool_moba_k = _kv_pool_t(cache_moba_k)
    pool_moba_v = _kv_pool_t(cache_moba_v)
    pool_fox_k = _kv_pool_t(cache_fox_k)
    pool_fox_v = _kv_pool_t(cache_fox_v)
    pool_fox_lf = jnp.transpose(cache_fox_logf, (0, 1, 3, 2))
    pool_mla_kpe = jnp.transpose(cache_mla_kpe, (0, 1, 3, 2))
    row1 = lambda v: v.reshape(1, -1)

    outs = {k: [] for k in ("ckv_p", "ckv_s", "kpe_p", "kpe_s", "mk_p", "mk_s", "mv_p", "mv_s",
                            "fk_p", "fk_s", "fv_p", "fv_s", "fl_p", "fl_s")}
    for l in range(depth):
        final = l == depth - 1
        ffn = (row1(ffn_norm[l]), ffn_w_gate[l].astype(BF16), ffn_w_up[l].astype(BF16),
               ffn_w_down[l].astype(BF16), row1(final_norm))
        if l % 2 == 0:
            e = l // 2
            w_in_x, w_uq_x, uk_bd, uv_bd = _prep_even(w_in_even[e], mla_w_uq[e], mla_w_uk[e], mla_w_uv[e])
            w_out = w_out_even[e].astype(BF16)
            common = (row1(attn_norm[l]), w_in_x, row1(mla_q_norm[e]), row1(mla_kv_norm[e]), w_uq_x, uk_bd)
            qcat, kcat, ckv, kpe, mq, mk, mv = _proj_even(xp, *common, cs_p, sn_p, tmp, BF16)
            mix_a = _mla_prompt(qcat, kcat, uv_bd, nb, seq)
            mix_b = _moba_prompt(mq, mk, mv, slopes_p, nb, seq)
            xp = _out_ffn(xp, mix_a, mix_b, w_out, *ffn, tmp, final)
            outs["ckv_p"].append(ckv.reshape(nb, seq, MLA_KV_RANK))
            outs["kpe_p"].append(kpe.reshape(nb, seq, MLA_ROPE))
            outs["mk_p"].append(mk.reshape(nb, seq, MOBA_HEADS, HEAD_DIM))
            outs["mv_p"].append(mv.reshape(nb, seq, MOBA_HEADS, HEAD_DIM))

            qcat, kcat, ckv, kpe, mq, mk, mv = _proj_even(xs, *common, cs_s, sn_s, tms, F32)
            mix_a = _mla_decode(qcat, kcat, uv_bd, cache_mla_ckv, pool_mla_kpe, e, page_table, t_new)
            means_t = _moba_means(pool_moba_k, e, page_table)
            mix_b = _moba_decode(mq, mk, mv, means_t, slopes_s, pool_moba_k, pool_moba_v, e, page_table, t_new)
            xs = _out_ffn(xs, mix_a, mix_b, w_out, *ffn, tms, final)
            outs["ckv_s"].append(ckv.reshape(nreq, t_new, MLA_KV_RANK))
            outs["kpe_s"].append(kpe.reshape(nreq, t_new, MLA_ROPE))
            outs["mk_s"].append(mk.reshape(nreq, t_new, MOBA_HEADS, HEAD_DIM))
            outs["mv_s"].append(mv.reshape(nreq, t_new, MOBA_HEADS, HEAD_DIM))
        else:
            o = l // 2
            w_in_x, w_out = _prep_odd(w_in_odd[o], w_out_odd[o])
            common = (row1(attn_norm[l]), w_in_x, row1(fox_b_forget[o]))
            q, k, v, lf = _proj_odd(xp, *common, tmp, BF16)
            f, ft = _fox_cumsum(lf, lf.reshape(nb, seq, FOX_HEADS).transpose(0, 2, 1), nb, seq)
            fq = f.reshape(mp, 4, 4).transpose(1, 0, 2)
            mix = _fox_prompt(q, k, v, fq, ft.reshape(nb, 4, 4, seq), nb, seq)
            xp = _out_ffn(xp, mix[0], mix[1], w_out, *ffn, tmp, final)
            outs["fk_p"].append(k.reshape(nb, seq, FOX_KV_HEADS, HEAD_DIM))
            outs["fv_p"].append(v.reshape(nb, seq, FOX_KV_HEADS, HEAD_DIM))
            outs["fl_p"].append(lf.reshape(nb, seq, FOX_HEADS))

            q, k, v, lf = _proj_odd(xs, *common, tms, F32)
            mix = _fox_decode(q, k, v, lf, pool_fox_k, pool_fox_v, pool_fox_lf, o, page_table, t_new)
            xs = _out_ffn(xs, mix[0], mix[1], w_out, *ffn, tms, final)
            outs["fk_s"].append(k.reshape(nreq, t_new, FOX_KV_HEADS, HEAD_DIM))
            outs["fv_s"].append(v.reshape(nreq, t_new, FOX_KV_HEADS, HEAD_DIM))
            outs["fl_s"].append(lf.reshape(nreq, t_new, FOX_HEADS))

    st = lambda name: jnp.stack(outs[name])
    return (xp.reshape(nb, seq, d), xs.reshape(nreq, t_new, d),
            st("ckv_p"), st("ckv_s"), st("kpe_p"), st("kpe_s"), st("mk_p"), st("mk_s"), st("mv_p"), st("mv_s"),
            st("fk_p"), st("fk_s"), st("fv_p"), st("fv_s"), st("fl_p"), st("fl_s"))
```

```python
import functools

import jax
import jax.numpy as jnp
from jax import lax
from jax.experimental import pallas as pl
from jax.experimental.pallas import tpu as pltpu

F32 = jnp.float32
BF16 = jnp.bfloat16

D_MODEL = 1024
HEAD_DIM = 64
PAGE = 128
RMS_EPS = 1e-6
MLA_HEADS = 8
MLA_Q_RANK = 256
MLA_KV_RANK = 128
MLA_NOPE = 64
MLA_ROPE = 32
MLA_V = 64
ROPE_THETA = 10000.0
MLA_SCALE = (MLA_NOPE + MLA_ROPE) ** -0.5
MOBA_HEADS = 8
MOBA_BLOCK = 256
MOBA_TOPK = 3
MOBA_SCALE = HEAD_DIM ** -0.5
FOX_HEADS = 16
FOX_KV_HEADS = 8
FOX_SCALE = HEAD_DIM ** -0.5
FFN_HIDDEN = 2816
FFN_CHUNK = 1408
LANES = 128
KV_WIDTH = 512
NEG = -1e30
VMEM_LIMIT = 56 * 1024 * 1024
PAGES_PER_STEP = 32
MLA_PAGES_PER_STEP = 64

HI = lax.Precision.HIGHEST


def _nt(a, b, precision=None):
    return lax.dot_general(a, b, (((1,), (1,)), ((), ())), precision=precision,
                           preferred_element_type=F32)


def _dot(a, b, precision=None):
    return jnp.dot(a, b, precision=precision, preferred_element_type=F32)


def _rms(x, g):
    return x * lax.rsqrt(jnp.mean(x * x, axis=-1, keepdims=True) + RMS_EPS) * g


def _params(*sem):
    return pltpu.CompilerParams(dimension_semantics=sem, vmem_limit_bytes=VMEM_LIMIT)


def _const_spec(shape):
    nd = len(shape)
    return pl.BlockSpec(shape, lambda *a: (0,) * nd, pipeline_mode=pl.Buffered(1))


def _iota(shape, dim):
    return lax.broadcasted_iota(jnp.int32, shape, dim)


def _proj_even_kernel(x_ref, g_ref, wlat_ref, wpe_ref, wmoba_ref, qn_ref, kvn_ref, wuq_ref, wuk_ref, cs_ref, sn_ref,
                      qcat_ref, kcat_ref, ckv_ref, kpe_ref, mq_ref, mk_ref, mv_ref):
    h = _rms(x_ref[...], g_ref[...]).astype(BF16)
    cq = _rms(_dot(h, wlat_ref[:, 0:256]), qn_ref[...]).astype(BF16)
    ckv = _rms(_dot(h, wlat_ref[:, 256:384]), kvn_ref[...])
    cs = cs_ref[...]
    sn = sn_ref[...]
    kpe = _dot(h, wpe_ref[:, 0:128]) * cs + _dot(h, wpe_ref[:, 128:256]) * sn
    ckv_ref[...] = ckv
    kpe_ref[...] = kpe[:, :MLA_ROPE]
    kcat_ref[:, 0:128] = ckv.astype(kcat_ref.dtype)
    kcat_ref[:, 128:256] = kpe.astype(kcat_ref.dtype)
    mq_ref[...] = _dot(h, wmoba_ref[:, 0:512])
    mk_ref[...] = _dot(h, wmoba_ref[:, 512:1024])
    mv_ref[...] = _dot(h, wmoba_ref[:, 1024:1536])
    qn = _dot(cq, wuq_ref[:, 0:512]).astype(BF16)
    for p in range(MLA_HEADS // 2):
        ql = _dot(qn[:, 128 * p:128 * p + 128], wuk_ref[p])
        for hh in range(2):
            hd = 2 * p + hh
            qpe = (_dot(cq, wuq_ref[:, 512 + 128 * hd:640 + 128 * hd]) * cs
                   + _dot(cq, wuq_ref[:, 1536 + 128 * hd:1664 + 128 * hd]) * sn)
            qcat_ref[:, 256 * hd:256 * hd + 128] = (ql[:, 128 * hh:128 * hh + 128] * MLA_SCALE).astype(qcat_ref.dtype)
            qcat_ref[:, 256 * hd + 128:256 * hd + 256] = (qpe * MLA_SCALE).astype(qcat_ref.dtype)


def _proj_even(x, g, w_lat, w_pe, w_moba, qn, kvn, w_uq, w_uk, cs, sn, tm, act_dtype):
    m = x.shape[0]
    row = lambda w: pl.BlockSpec((tm, w), lambda i: (i, 0))
    out_shape = (
        jax.ShapeDtypeStruct((m, 2048), act_dtype),
        jax.ShapeDtypeStruct((m, 256), act_dtype),
        jax.ShapeDtypeStruct((m, 128), F32),
        jax.ShapeDtypeStruct((m, 32), F32),
        jax.ShapeDtypeStruct((m, 512), F32),
        jax.ShapeDtypeStruct((m, 512), F32),
        jax.ShapeDtypeStruct((m, 512), F32),
    )
    return pl.pallas_call(
        _proj_even_kernel,
        grid=(m // tm,),
        in_specs=[row(D_MODEL), _const_spec((1, D_MODEL)), _const_spec(w_lat.shape), _const_spec(w_pe.shape),
                  _const_spec(w_moba.shape), _const_spec((1, 256)), _const_spec((1, 128)),
                  _const_spec(w_uq.shape), _const_spec(w_uk.shape), row(128), row(128)],
        out_specs=(row(2048), row(256), row(128), row(32), row(512), row(512), row(512)),
        out_shape=out_shape,
        compiler_params=_params("parallel"),
        name="proj_even",
    )(x, g, w_lat, w_pe, w_moba, qn, kvn, w_uq, w_uk, cs, sn)


def _proj_odd_kernel(x_ref, g_ref, wq_ref, wkv_ref, wf_ref, bf_ref, q_ref, k_ref, v_ref, lf_ref):
    h = _rms(x_ref[...], g_ref[...]).astype(BF16)
    q_ref[...] = _dot(h, wq_ref[...]).astype(q_ref.dtype)
    k_ref[...] = _dot(h, wkv_ref[:, 0:512])
    v_ref[...] = _dot(h, wkv_ref[:, 512:1024])
    zf = _dot(h, wf_ref[...])[:, :FOX_HEADS] + bf_ref[...]
    lf_ref[...] = -(jnp.maximum(-zf, 0.0) + jnp.log1p(jnp.exp(-jnp.abs(zf))))


def _proj_odd(x, g, w_q, w_kv, w_f, b_f, tm, act_dtype):
    m = x.shape[0]
    row = lambda w: pl.BlockSpec((tm, w), lambda i: (i, 0))
    out_shape = (
        jax.ShapeDtypeStruct((m, 1024), act_dtype),
        jax.ShapeDtypeStruct((m, 512), F32),
        jax.ShapeDtypeStruct((m, 512), F32),
        jax.ShapeDtypeStruct((m, FOX_HEADS), F32),
    )
    return pl.pallas_call(
        _proj_odd_kernel,
        grid=(m // tm,),
        in_specs=[row(D_MODEL), _const_spec((1, D_MODEL)), _const_spec(w_q.shape), _const_spec(w_kv.shape),
                  _const_spec(w_f.shape), _const_spec((1, FOX_HEADS))],
        out_specs=(row(1024), row(512), row(512), row(FOX_HEADS)),
        out_shape=out_shape,
        compiler_params=_params("parallel"),
        name="proj_odd",
    )(x, g, w_q, w_kv, w_f, b_f)


def _out_ffn_kernel(x_ref, a_ref, b_ref, wo_ref, g_ref, wg_ref, wu_ref, wd_ref, fn_ref, o_ref, *, final):
    x1 = (x_ref[...] + _dot(a_ref[...].astype(BF16), wo_ref[0:512, :])
          + _dot(b_ref[...].astype(BF16), wo_ref[512:1024, :]))
    h = _rms(x1, g_ref[...]).astype(BF16)
    acc = jnp.zeros_like(x1)
    for c in range(FFN_HIDDEN // FFN_CHUNK):
        lo = c * FFN_CHUNK
        gate = _dot(h, wg_ref[:, lo:lo + FFN_CHUNK])
        up = _dot(h, wu_ref[:, lo:lo + FFN_CHUNK])
        act = (gate / (1.0 + jnp.exp(-gate))) * up
        acc = acc + _dot(act.astype(BF16), wd_ref[lo:lo + FFN_CHUNK, :])
    out = x1 + acc
    if final:
        out = _rms(out, fn_ref[...])
    o_ref[...] = out


def _out_ffn(x, mix_a, mix_b, w_out, g, w_gate, w_up, w_down, final_norm, tm, final):
    m = x.shape[0]
    row = lambda w: pl.BlockSpec((tm, w), lambda i: (i, 0))
    return pl.pallas_call(
        functools.partial(_out_ffn_kernel, final=final),
        grid=(m // tm,),
        in_specs=[row(D_MODEL), row(512), row(512), _const_spec(w_out.shape), _const_spec((1, D_MODEL)),
                  _const_spec(w_gate.shape), _const_spec(w_up.shape), _const_spec(w_down.shape),
                  _const_spec((1, D_MODEL))],
        out_specs=row(D_MODEL),
        out_shape=jax.ShapeDtypeStruct((m, D_MODEL), F32),
        compiler_params=_params("parallel"),
        name="out_ffn",
    )(x, mix_a, mix_b, w_out, g, w_gate, w_up, w_down, final_norm)


def _init_stats(m_ref, l_ref, acc_ref):
    m_ref[...] = jnp.full(m_ref.shape, NEG, F32)
    l_ref[...] = jnp.zeros(l_ref.shape, F32)
    acc_ref[...] = jnp.zeros(acc_ref.shape, F32)


def _softmax_tile(tiles, rows, m_ref, l_ref, p_ref, acc_ref, alpha_ref=None):
    mx = tiles[0]
    for t in tiles[1:]:
        mx = jnp.maximum(mx, t)
    m_old = m_ref[rows, :]
    m_new = jnp.maximum(m_old, jnp.max(mx, axis=-1, keepdims=True))
    alpha = jnp.exp(m_old - m_new)
    ps = None
    for c, t in enumerate(tiles):
        p = jnp.exp(t - m_new)
        ps = p if ps is None else ps + p
        p_ref[rows, LANES * c:LANES * (c + 1)] = p.astype(p_ref.dtype)
    l_ref[rows, :] = alpha * l_ref[rows, :] + jnp.sum(ps, axis=-1, keepdims=True)
    m_ref[rows, :] = m_new
    if alpha_ref is not None:
        alpha_ref[rows, :] = alpha
        return
    for c in range(acc_ref.shape[1] // LANES):
        acc_ref[rows, LANES * c:LANES * (c + 1)] = acc_ref[rows, LANES * c:LANES * (c + 1)] * alpha


def _causal_blocks(n_plain, qk, softmax, pv):
    qk(0, 0)

    def pair(jj, carry):
        j = 2 * jj
        qk(j + 1, 1)
        softmax(j, 0, False)
        pv(j, 0)
        qk(j + 2, 0)
        softmax(j + 1, 1, False)
        pv(j + 1, 1)
        return carry

    lax.fori_loop(0, n_plain // 2, pair, 0)

    @pl.when(n_plain % 2 == 0)
    def _():
        softmax(n_plain, 0, True)
        pv(n_plain, 0)

    @pl.when(n_plain % 2 == 1)
    def _():
        qk(n_plain, 1)
        softmax(n_plain - 1, 0, False)
        pv(n_plain - 1, 0)
        softmax(n_plain, 1, True)
        pv(n_plain, 1)


def _flash_scratch(rows, tk, width):
    return [pltpu.VMEM((rows, tk), F32), pltpu.VMEM((rows, tk), F32),
            pltpu.VMEM((rows, tk), BF16), pltpu.VMEM((rows, tk), BF16),
            pltpu.VMEM((rows, LANES), F32), pltpu.VMEM((rows, LANES), F32),
            pltpu.VMEM((rows, LANES), F32), pltpu.VMEM((rows, LANES), F32), pltpu.VMEM((rows, width), F32)]


def _mla_prompt_kernel(q_ref, k_ref, wuv_ref, o_ref, qs_ref, s0_ref, s1_ref, p0_ref, p1_ref, a0_ref, a1_ref,
                       m_ref, l_ref, acc_ref, *, tq, tk, rt):
    i = pl.program_id(1)
    nh = MLA_HEADS
    nl = tk // LANES
    s_refs, p_refs, a_refs = (s0_ref, s1_ref), (p0_ref, p1_ref), (a0_ref, a1_ref)
    for h in range(nh):
        qs_ref[h * tq:(h + 1) * tq, :] = q_ref[:, 256 * h:256 * h + 256]
    _init_stats(m_ref, l_ref, acc_ref)
    j_last = (i * tq + tq - 1) // tk

    def qk(j, buf):
        kb = k_ref[pl.ds(pl.multiple_of(j * tk, tk), tk), :]
        s_refs[buf][...] = _nt(qs_ref[...], kb)

    def pv(j, buf):
        vb = k_ref[pl.ds(pl.multiple_of(j * tk, tk), tk), 0:MLA_KV_RANK]
        acc_ref[...] = acc_ref[...] * a_refs[buf][...] + _dot(p_refs[buf][...], vb)

    def softmax(j, buf, masked):
        for t in range(nh * tq // rt):
            rows = slice(t * rt, (t + 1) * rt)
            tiles = []
            for c in range(nl):
                s = s_refs[buf][rows, LANES * c:LANES * (c + 1)]
                if masked:
                    qpos = i * tq + (t * rt) % tq + _iota((rt, 1), 0)
                    kpos = j * tk + LANES * c + _iota((1, LANES), 1)
                    s = jnp.where(kpos <= qpos, s, NEG)
                tiles.append(s)
            _softmax_tile(tiles, rows, m_ref, l_ref, p_refs[buf], acc_ref, a_refs[buf])

    _causal_blocks(j_last, qk, softmax, pv)
    o = acc_ref[...] / l_ref[...]
    for p in range(nh // 2):
        pair = jnp.concatenate([o[(2 * p) * tq:(2 * p + 1) * tq], o[(2 * p + 1) * tq:(2 * p + 2) * tq]], axis=1)
        o_ref[:, 128 * p:128 * p + 128] = _dot(pair.astype(BF16), wuv_ref[p]).astype(o_ref.dtype)


def _mla_prompt(qcat, kcat, w_uv_bd, nb, seq):
    tq, tk, rt = 128, 256, 32
    nq = seq // tq
    rows = MLA_HEADS * tq
    return pl.pallas_call(
        functools.partial(_mla_prompt_kernel, tq=tq, tk=tk, rt=rt),
        grid=(nb, nq),
        in_specs=[pl.BlockSpec((tq, 2048), lambda b, i: (b * nq + i, 0)),
                  pl.BlockSpec((seq, 256), lambda b, i: (b, 0)),
                  _const_spec(w_uv_bd.shape)],
        out_specs=pl.BlockSpec((tq, 512), lambda b, i: (b * nq + i, 0)),
        out_shape=jax.ShapeDtypeStruct((nb * seq, 512), BF16),
        scratch_shapes=[pltpu.VMEM((rows, 256), BF16)] + _flash_scratch(rows, tk, MLA_KV_RANK),
        compiler_params=_params("parallel", "arbitrary"),
        name="mla_prompt",
    )(qcat, kcat, w_uv_bd)


def _top3_mask(scores, n_valid):
    rows, nblk = scores.shape
    lane = _iota((rows, nblk), 1).astype(F32)
    avail = jnp.where(lane < n_valid, 1.0, 0.0)
    sel = jnp.zeros((rows, nblk), F32)
    for _ in range(MOBA_TOPK):
        work = jnp.where(avail > 0.0, scores, -jnp.inf)
        mx = jnp.max(work, axis=-1, keepdims=True)
        cand = jnp.where(work == mx, jnp.where(avail > 0.0, lane, float(nblk)), float(nblk))
        idx = jnp.min(cand, axis=-1, keepdims=True)
        pick = lane == idx
        sel = jnp.where(pick, 1.0, sel)
        avail = jnp.where(pick, 0.0, avail)
    return sel


def _moba_prompt_kernel(q_ref, k_ref, v_ref, slope_ref, o_ref, mean_ref, qbd_ref, sel_ref, ab_ref,
                        s0_ref, s1_ref, p0_ref, p1_ref, a0_ref, a1_ref, m_ref, l_ref, acc_ref, *, tq, nblk, rt):
    i = pl.program_id(2)
    blk = MOBA_BLOCK
    nl = blk // LANES
    rows_all = 2 * tq
    s_refs, p_refs, a_refs = (s0_ref, s1_ref), (p0_ref, p1_ref), (a0_ref, a1_ref)

    @pl.when(i == 0)
    def _():
        for j in range(nblk):
            mean_ref[j:j + 1, :] = jnp.sum(k_ref[j * blk:(j + 1) * blk, :], axis=0, keepdims=True) * (1.0 / blk)

    q = q_ref[...]
    lane = _iota((tq, LANES), 1)
    qbd = jnp.concatenate([jnp.where(lane < HEAD_DIM, q, 0.0), jnp.where(lane >= HEAD_DIM, q, 0.0)], axis=0)
    qbd_ref[...] = (qbd * MOBA_SCALE).astype(BF16)
    sel_ref[...] = _top3_mask(_nt(qbd, mean_ref[...], precision=HI), i)
    qposf = (i * tq + _iota((rows_all, 1), 0) % tq).astype(F32)
    ab_ref[...] = jnp.broadcast_to(-slope_ref[0] * qposf, (rows_all, LANES))
    _init_stats(m_ref, l_ref, acc_ref)

    def qk(j, buf):
        kb = k_ref[pl.ds(pl.multiple_of(j * blk, blk), blk), :].astype(BF16)
        s_refs[buf][...] = _nt(qbd_ref[...], kb)

    def pv(j, buf):
        vb = v_ref[pl.ds(pl.multiple_of(j * blk, blk), blk), :].astype(BF16)
        acc_ref[...] = acc_ref[...] * a_refs[buf][...] + _dot(p_refs[buf][...], vb)

    def softmax(j, buf, own):
        for g in range(2):
            sg = slope_ref[0, g * tq:g * tq + 1, :]
            cols = [sg * (j * blk + LANES * c + _iota((1, LANES), 1)).astype(F32) for c in range(nl)]
            for t in range(tq // rt):
                r0 = g * tq + t * rt
                rows = slice(r0, r0 + rt)
                rb = ab_ref[rows, :]
                if not own:
                    chosen = jnp.sum(jnp.where(_iota((rt, nblk), 1) == j, sel_ref[rows, :], 0.0),
                                     axis=-1, keepdims=True)
                    rb = rb + jnp.where(chosen > 0.0, 0.0, NEG)
                tiles = []
                for c in range(nl):
                    s = s_refs[buf][rows, LANES * c:LANES * (c + 1)] + (rb + cols[c])
                    if own:
                        s = jnp.where(LANES * c + _iota((1, LANES), 1) <= t * rt + _iota((rt, 1), 0), s, NEG)
                    tiles.append(s)
                _softmax_tile(tiles, rows, m_ref, l_ref, p_refs[buf], acc_ref, a_refs[buf])

    qk(i, 0)
    qk(0, 1)
    softmax(i, 0, True)
    pv(i, 0)

    def pair(jj, carry):
        j = 2 * jj
        qk(j + 1, 0)
        softmax(j, 1, False)
        pv(j, 1)
        qk(j + 2, 1)
        softmax(j + 1, 0, False)
        pv(j + 1, 0)
        return carry

    lax.fori_loop(0, i // 2, pair, 0)

    @pl.when(i % 2 == 1)
    def _():
        softmax(i - 1, 1, False)
        pv(i - 1, 1)

    o = acc_ref[...] / l_ref[...]
    o_ref[...] = jnp.where(lane < HEAD_DIM, o[0:tq], o[tq:2 * tq]).astype(o_ref.dtype)


def _moba_prompt(mq, mk, mv, slopes, nb, seq):
    tq, rt = MOBA_BLOCK, 32
    nq = seq // tq
    nblk = seq // MOBA_BLOCK
    npair = MOBA_HEADS // 2
    rows = 2 * tq
    return pl.pallas_call(
        functools.partial(_moba_prompt_kernel, tq=tq, nblk=nblk, rt=rt),
        grid=(nb, npair, nq),
        in_specs=[pl.BlockSpec((tq, LANES), lambda b, p, i: (b * nq + i, p)),
                  pl.BlockSpec((seq, LANES), lambda b, p, i: (b, p)),
                  pl.BlockSpec((seq, LANES), lambda b, p, i: (b, p)),
                  pl.BlockSpec((1, rows, 1), lambda b, p, i: (p, 0, 0))],
        out_specs=pl.BlockSpec((tq, LANES), lambda b, p, i: (b * nq + i, p)),
        out_shape=jax.ShapeDtypeStruct((nb * seq, 512), BF16),
        scratch_shapes=[pltpu.VMEM((nblk, LANES), F32), pltpu.VMEM((rows, LANES), BF16),
                        pltpu.VMEM((rows, nblk), F32), pltpu.VMEM((rows, LANES), F32)]
        + _flash_scratch(rows, MOBA_BLOCK, LANES),
        compiler_params=_params("parallel", "parallel", "arbitrary"),
        name="moba_prompt",
    )(mq, mk, mv, slopes)


def _fox_cumsum_kernel(lf_ref, lft_ref, f_ref, ft_ref, *, seq, ch):
    r = _iota((ch, ch), 0)
    c = _iota((ch, ch), 1)
    lower = (c <= r).astype(F32)
    upper = (r <= c).astype(F32)
    carry = jnp.zeros((1, FOX_HEADS), F32)
    carry_t = jnp.zeros((FOX_HEADS, 1), F32)
    for j in range(seq // ch):
        f = _dot(lower, lf_ref[j * ch:(j + 1) * ch, :], precision=HI) + carry
        f_ref[j * ch:(j + 1) * ch, :] = f
        carry = f[ch - 1:ch, :]
        ft = _dot(lft_ref[0, :, j * ch:(j + 1) * ch], upper, precision=HI) + carry_t
        ft_ref[0, :, j * ch:(j + 1) * ch] = ft
        carry_t = ft[:, ch - 1:ch]


def _fox_cumsum(logf, logf_t, nb, seq):
    return pl.pallas_call(
        functools.partial(_fox_cumsum_kernel, seq=seq, ch=256),
        grid=(nb,),
        in_specs=[pl.BlockSpec((seq, FOX_HEADS), lambda b: (b, 0)),
                  pl.BlockSpec((1, FOX_HEADS, seq), lambda b: (b, 0, 0))],
        out_specs=(pl.BlockSpec((seq, FOX_HEADS), lambda b: (b, 0)),
                   pl.BlockSpec((1, FOX_HEADS, seq), lambda b: (b, 0, 0))),
        out_shape=(jax.ShapeDtypeStruct((nb * seq, FOX_HEADS), F32),
                   jax.ShapeDtypeStruct((nb, FOX_HEADS, seq), F32)),
        compiler_params=_params("parallel"),
        name="fox_cumsum",
    )(logf, logf_t)


def _fox_prompt_kernel(q0_ref, q1_ref, k_ref, v_ref, fq_ref, fk_ref, o_ref, qbd_ref, fqr_ref,
                       s0_ref, s1_ref, p0_ref, p1_ref, a0_ref, a1_ref, m_ref, l_ref, acc_ref, *, tq, rt):
    i = pl.program_id(2)
    tk = tq
    nl = tk // LANES
    s_refs, p_refs, a_refs = (s0_ref, s1_ref), (p0_ref, p1_ref), (a0_ref, a1_ref)
    lane = _iota((tq, LANES), 1)
    lo = lane < HEAD_DIM
    q0 = q0_ref[...] * FOX_SCALE
    q1 = q1_ref[...] * FOX_SCALE
    zero = jnp.zeros_like(q0)
    for r, blkq in enumerate((jnp.where(lo, q0, zero), jnp.where(lo, q1, zero),
                              jnp.where(lo, zero, q0), jnp.where(lo, zero, q1))):
        qbd_ref[r * tq:(r + 1) * tq, :] = blkq
        fqr_ref[r * tq:(r + 1) * tq, :] = jnp.broadcast_to(fq_ref[0, :, r:r + 1], (tq, LANES))
    _init_stats(m_ref, l_ref, acc_ref)

    def qk(j, buf):
        kb = k_ref[pl.ds(pl.multiple_of(j * tk, tk), tk), :].astype(BF16)
        s_refs[buf][...] = _nt(qbd_ref[...], kb)

    def pv(j, buf):
        vb = v_ref[pl.ds(pl.multiple_of(j * tk, tk), tk), :].astype(BF16)
        acc_ref[...] = acc_ref[...] * a_refs[buf][...] + _dot(p_refs[buf][...], vb)

    def softmax(j, buf, masked):
        for r in range(4):
            fk = [fk_ref[0, 0, r:r + 1, pl.ds(pl.multiple_of(j * tk + LANES * c, LANES), LANES)] for c in range(nl)]
            for t in range(tq // rt):
                r0 = r * tq + t * rt
                rows = slice(r0, r0 + rt)
                fq = fqr_ref[rows, :]
                tiles = []
                for c in range(nl):
                    s = s_refs[buf][rows, LANES * c:LANES * (c + 1)] + (fq - fk[c])
                    if masked:
                        s = jnp.where(LANES * c + _iota((1, LANES), 1) <= t * rt + _iota((rt, 1), 0), s, NEG)
                    tiles.append(s)
                _softmax_tile(tiles, rows, m_ref, l_ref, p_refs[buf], acc_ref, a_refs[buf])

    _causal_blocks(i, qk, softmax, pv)
    o = acc_ref[...] / l_ref[...]
    o_ref[0] = jnp.where(lo, o[0:tq], o[2 * tq:3 * tq]).astype(o_ref.dtype)
    o_ref[1] = jnp.where(lo, o[tq:2 * tq], o[3 * tq:4 * tq]).astype(o_ref.dtype)


def _fox_prompt(q, k, v, fq, fk, nb, seq):
    tq, rt = 256, 32
    nq = seq // tq
    npair = FOX_KV_HEADS // 2
    rows = 4 * tq
    return pl.pallas_call(
        functools.partial(_fox_prompt_kernel, tq=tq, rt=rt),
        grid=(nb, npair, nq),
        in_specs=[pl.BlockSpec((tq, LANES), lambda b, p, i: (b * nq + i, p)),
                  pl.BlockSpec((tq, LANES), lambda b, p, i: (b * nq + i, npair + p)),
                  pl.BlockSpec((seq, LANES), lambda b, p, i: (b, p)),
                  pl.BlockSpec((seq, LANES), lambda b, p, i: (b, p)),
                  pl.BlockSpec((1, tq, 4), lambda b, p, i: (p, b * nq + i, 0)),
                  pl.BlockSpec((1, 1, 4, seq), lambda b, p, i: (b, p, 0, 0))],
        out_specs=pl.BlockSpec((2, tq, LANES), lambda b, p, i: (0, b * nq + i, p)),
        out_shape=jax.ShapeDtypeStruct((2, nb * seq, 512), BF16),
        scratch_shapes=[pltpu.VMEM((rows, LANES), BF16), pltpu.VMEM((rows, LANES), F32)]
        + _flash_scratch(rows, tq, LANES),
        compiler_params=_params("parallel", "parallel", "arbitrary"),
        name="fox_prompt",
    )(q, q, k, v, fq, fk)


def _page_specs(block, npg, layer, reverse_chunks=None):
    specs = []
    for pg in range(npg):
        if reverse_chunks is None:
            imap = lambda b, c, pt, pg=pg: (layer, pt[b, c * npg + pg], 0, 0)
        else:
            imap = lambda b, c, pt, pg=pg: (layer, pt[b, (reverse_chunks - 1 - c) * npg + pg], 0, 0)
        specs.append(pl.BlockSpec((1, 1) + block, imap))
    return specs


def _new_token_init(qbd16, kn, vn, bias_new, m_ref, l_ref, acc_ref, t_new):
    rows = qbd16.shape[0]
    s = _nt(qbd16, kn.astype(BF16))
    if bias_new is not None:
        s = s + bias_new
    tq = _iota((rows, t_new), 0) % t_new
    tk = _iota((rows, t_new), 1)
    s = jnp.where(tk <= tq, s, NEG)
    m = jnp.max(s, axis=-1, keepdims=True)
    p = jnp.exp(s - m)
    acc = jnp.zeros((rows, vn.shape[1]), F32)
    for t in range(t_new):
        acc = acc + p[:, t:t + 1] * vn[t:t + 1, :]
    m_ref[...] = jnp.broadcast_to(m, m_ref.shape)
    l_ref[...] = jnp.broadcast_to(jnp.sum(p, axis=-1, keepdims=True), l_ref.shape)
    acc_ref[...] = acc


def _mla_decode_kernel(pt_ref, q_ref, kn_ref, wuv_ref, *rest, npg, t_new, rt):
    ckv_refs = rest[:npg]
    kpe_refs = rest[npg:2 * npg]
    o_ref, qs_ref, ck_ref, kpt_ref, s_ref, p_ref, m_ref, l_ref, acc_ref = rest[2 * npg:]
    c = pl.program_id(1)
    nh = MLA_HEADS
    rows_all = nh * t_new
    tk = npg * PAGE

    @pl.when(c == 0)
    def _():
        q = q_ref[...]
        qs = jnp.concatenate([q[:, 256 * h:256 * h + 256] for h in range(nh)], axis=0).astype(BF16)
        qs_ref[...] = qs
        kn = kn_ref[...]
        _new_token_init(qs, kn, kn[:, 0:MLA_KV_RANK], None, m_ref, l_ref, acc_ref, t_new)

    for pg in range(npg):
        ck_ref[PAGE * pg:PAGE * (pg + 1), :] = ckv_refs[pg][0, 0].astype(BF16)
        kpt_ref[:, PAGE * pg:PAGE * (pg + 1)] = kpe_refs[pg][0, 0].astype(BF16)
    s_ref[...] = (_nt(qs_ref[:, 0:MLA_KV_RANK], ck_ref[...])
                  + _dot(qs_ref[:, MLA_KV_RANK:MLA_KV_RANK + MLA_ROPE], kpt_ref[...]))
    for t in range(rows_all // rt):
        rows = slice(t * rt, (t + 1) * rt)
        tiles = [s_ref[rows, LANES * cl:LANES * (cl + 1)] for cl in range(tk // LANES)]
        _softmax_tile(tiles, rows, m_ref, l_ref, p_ref, acc_ref)
    acc_ref[...] += _dot(p_ref[...], ck_ref[...])

    @pl.when(c == pl.num_programs(1) - 1)
    def _():
        o = acc_ref[...] / l_ref[...]
        for p in range(nh // 2):
            pair = jnp.concatenate([o[(2 * p) * t_new:(2 * p + 1) * t_new],
                                    o[(2 * p + 1) * t_new:(2 * p + 2) * t_new]], axis=1)
            o_ref[:, 128 * p:128 * p + 128] = _dot(pair.astype(BF16), wuv_ref[p]).astype(o_ref.dtype)


def _mla_decode(qcat, kcat, w_uv_bd, pool_ckv, pool_kpe_t, layer, page_table, t_new):
    nreq, n_pages = page_table.shape
    npg = min(MLA_PAGES_PER_STEP, n_pages)
    nchunk = n_pages // npg
    rows = MLA_HEADS * t_new
    tk = npg * PAGE
    grid_spec = pltpu.PrefetchScalarGridSpec(
        num_scalar_prefetch=1,
        grid=(nreq, nchunk),
        in_specs=[pl.BlockSpec((t_new, 2048), lambda b, c, pt: (b, 0)),
                  pl.BlockSpec((t_new, 256), lambda b, c, pt: (b, 0)),
                  pl.BlockSpec(w_uv_bd.shape, lambda b, c, pt: (0, 0, 0))]
        + _page_specs((PAGE, MLA_KV_RANK), npg, layer) + _page_specs((MLA_ROPE, PAGE), npg, layer),
        out_specs=pl.BlockSpec((t_new, 512), lambda b, c, pt: (b, 0)),
        scratch_shapes=[pltpu.VMEM((rows, 256), BF16), pltpu.VMEM((tk, MLA_KV_RANK), BF16),
                        pltpu.VMEM((MLA_ROPE, tk), BF16), pltpu.VMEM((rows, tk), F32), pltpu.VMEM((rows, tk), BF16),
                        pltpu.VMEM((rows, LANES), F32), pltpu.VMEM((rows, LANES), F32),
                        pltpu.VMEM((rows, MLA_KV_RANK), F32)],
    )
    return pl.pallas_call(
        functools.partial(_mla_decode_kernel, npg=npg, t_new=t_new, rt=16),
        grid_spec=grid_spec,
        out_shape=jax.ShapeDtypeStruct((nreq * t_new, 512), F32),
        compiler_params=_params("parallel", "arbitrary"),
        name="mla_decode",
    )(page_table, qcat, kcat, w_uv_bd, *([pool_ckv] * npg), *([pool_kpe_t] * npg))


def _moba_means_kernel(pt_ref, *rest, npg):
    k_refs = rest[:npg]
    o_ref = rest[npg]
    c = pl.program_id(1)
    ppb = MOBA_BLOCK // PAGE
    bpc = npg // ppb
    nblk = o_ref.shape[2]

    @pl.when(c == 0)
    def _():
        o_ref[...] = jnp.zeros(o_ref.shape, F32)

    lane = _iota((KV_WIDTH, nblk), 1)
    acc = o_ref[0]
    for j in range(bpc):
        tot = k_refs[ppb * j][0, 0]
        for pg in range(1, ppb):
            tot = tot + k_refs[ppb * j + pg][0, 0]
        col = jnp.sum(tot, axis=1, keepdims=True) * (1.0 / MOBA_BLOCK)
        acc = jnp.where(lane == c * bpc + j, col, acc)
    o_ref[0] = acc


def _moba_means(pool_k_t, layer, page_table):
    nreq, n_pages = page_table.shape
    npg = min(PAGES_PER_STEP, n_pages)
    nchunk = n_pages // npg
    nblk = n_pages * PAGE // MOBA_BLOCK
    grid_spec = pltpu.PrefetchScalarGridSpec(
        num_scalar_prefetch=1,
        grid=(nreq, nchunk),
        in_specs=_page_specs((KV_WIDTH, PAGE), npg, layer),
        out_specs=pl.BlockSpec((1, KV_WIDTH, nblk), lambda b, c, pt: (b, 0, 0)),
    )
    return pl.pallas_call(
        functools.partial(_moba_means_kernel, npg=npg),
        grid_spec=grid_spec,
        out_shape=jax.ShapeDtypeStruct((nreq, KV_WIDTH, nblk), F32),
        compiler_params=_params("parallel", "arbitrary"),
        name="moba_means",
    )(page_table, *([pool_k_t] * npg))


def _head_mask(rows_per_group, n_groups, head_of_group):
    rows = n_groups * rows_per_group
    grp = _iota((rows, KV_WIDTH), 0) // rows_per_group
    blk = _iota((rows, KV_WIDTH), 1) // HEAD_DIM
    return blk == head_of_group(grp)


DECODE_PARTS = 2


def _decode_scratch(rows, tk):
    tkp = tk // DECODE_PARTS
    per_part = lambda shape, dt: [pltpu.VMEM(shape, dt) for _ in range(DECODE_PARTS)]
    return (per_part((KV_WIDTH, tkp), BF16) + per_part((KV_WIDTH, tkp), BF16) + per_part((rows, tkp), F32)
            + per_part((rows, tkp), BF16) + per_part((rows, LANES), F32)
            + [pltpu.VMEM((rows, LANES), F32), pltpu.VMEM((rows, LANES), F32), pltpu.VMEM((rows, KV_WIDTH), F32)])


def _decode_chunk(k_refs, v_refs, qbd_ref, scratch, tile_bias, rows_all, rt):
    np_ = DECODE_PARTS
    kt_refs, vt_refs, s_refs, p_refs, a_refs = (scratch[np_ * n:np_ * (n + 1)] for n in range(5))
    m_ref, l_ref, acc_ref = scratch[5 * np_:]
    ppp = len(k_refs) // np_
    nl = ppp * PAGE // LANES
    for h in range(np_):
        for pg in range(ppp):
            kt_refs[h][:, PAGE * pg:PAGE * (pg + 1)] = k_refs[h * ppp + pg][0, 0].astype(BF16)
            vt_refs[h][:, PAGE * pg:PAGE * (pg + 1)] = v_refs[h * ppp + pg][0, 0].astype(BF16)
        s_refs[h][...] = _dot(qbd_ref[...], kt_refs[h][...])
    for h in range(np_):
        for t in range(rows_all // rt):
            rows = slice(t * rt, (t + 1) * rt)
            tiles = [s_refs[h][rows, LANES * cl:LANES * (cl + 1)] + tile_bias(t, h * nl + cl) for cl in range(nl)]
            _softmax_tile(tiles, rows, m_ref, l_ref, p_refs[h], acc_ref, a_refs[h])
        pv = _nt(p_refs[h][...], vt_refs[h][...])
        alpha = a_refs[h][...]
        for cl in range(KV_WIDTH // LANES):
            lanes = slice(LANES * cl, LANES * (cl + 1))
            acc_ref[:, lanes] = acc_ref[:, lanes] * alpha + pv[:, lanes]


def _fox_decode_kernel(pt_ref, q_ref, kn_ref, vn_ref, lfn_ref, *rest, npg, t_new, rt):
    k_refs = rest[:npg]
    v_refs = rest[npg:2 * npg]
    lf_refs = rest[2 * npg:3 * npg]
    o_ref, qbd_ref, sfx_ref, carry_ref, gq_ref = rest[3 * npg:3 * npg + 5]
    scratch = rest[3 * npg + 5:]
    m_ref, l_ref, acc_ref = scratch[-3:]
    c = pl.program_id(1)
    nhq = FOX_HEADS
    rows_all = nhq * t_new
    tk = npg * PAGE
    hmask = _head_mask(t_new, nhq, lambda grp: grp // 2)

    @pl.when(c == 0)
    def _():
        q = q_ref[...] * FOX_SCALE
        qrep = jnp.concatenate([q[:, KV_WIDTH * (hq % 2):KV_WIDTH * (hq % 2 + 1)] for hq in range(nhq)], axis=0)
        qbd16 = jnp.where(hmask, qrep, 0.0).astype(BF16)
        qbd_ref[...] = qbd16
        eye = (_iota((nhq, nhq), 0) == _iota((nhq, nhq), 1)).astype(F32)
        lfn = lfn_ref[...]
        tri = (_iota((t_new, t_new), 1) <= _iota((t_new, t_new), 0)).astype(F32)
        gct = _nt(eye, _dot(tri, lfn, precision=HI), precision=HI)
        g = jnp.concatenate([jnp.broadcast_to(gct[hq:hq + 1, :], (t_new, t_new)) for hq in range(nhq)], axis=0)
        tq = _iota((rows_all, t_new), 0) % t_new
        tl = _iota((rows_all, t_new), 1)
        gq = jnp.sum(jnp.where(tl == tq, g, 0.0), axis=-1, keepdims=True)
        gq_ref[...] = jnp.broadcast_to(gq, gq_ref.shape)
        carry_ref[...] = jnp.zeros(carry_ref.shape, F32)
        _new_token_init(qbd16, kn_ref[...], vn_ref[...], gq - g, m_ref, l_ref, acc_ref, t_new)

    lft = jnp.concatenate([r[0, 0] for r in lf_refs], axis=1)
    lane = _iota((nhq, tk), 1)
    incl = lft
    sh = 1
    while sh < tk:
        incl = incl + jnp.where(lane < tk - sh, pltpu.roll(incl, tk - sh, axis=1), 0.0)
        sh *= 2
    sfx_ref[...] = incl - lft + carry_ref[...]
    carry_ref[...] = carry_ref[...] + incl[:, 0:1]
    hq_per_tile = rt // t_new

    def tile_bias(t, cl):
        lanes = slice(LANES * cl, LANES * (cl + 1))
        sfx = jnp.concatenate([jnp.broadcast_to(sfx_ref[hq:hq + 1, lanes], (t_new, LANES))
                               for hq in range(t * hq_per_tile, (t + 1) * hq_per_tile)], axis=0)
        return sfx + gq_ref[t * rt:(t + 1) * rt, :]

    _decode_chunk(k_refs, v_refs, qbd_ref, scratch, tile_bias, rows_all, rt)

    @pl.when(c == pl.num_programs(1) - 1)
    def _():
        o = jnp.where(hmask, acc_ref[...] / l_ref[:, 0:1], 0.0)
        for g in range(2):
            tot = jnp.zeros((t_new, KV_WIDTH), F32)
            for kvh in range(FOX_KV_HEADS):
                r0 = (kvh * 2 + g) * t_new
                tot = tot + o[r0:r0 + t_new, :]
            o_ref[g] = tot


def _fox_decode(q, k_new, v_new, lf_new, pool_k_t, pool_v_t, pool_lf_t, layer, page_table, t_new):
    nreq, n_pages = page_table.shape
    npg = min(PAGES_PER_STEP, n_pages)
    nchunk = n_pages // npg
    rows = FOX_HEADS * t_new
    tk = npg * PAGE
    new = lambda w: pl.BlockSpec((t_new, w), lambda b, c, pt: (b, 0))
    grid_spec = pltpu.PrefetchScalarGridSpec(
        num_scalar_prefetch=1,
        grid=(nreq, nchunk),
        in_specs=[new(1024), new(KV_WIDTH), new(KV_WIDTH), new(FOX_HEADS)]
        + _page_specs((KV_WIDTH, PAGE), npg, layer, nchunk) + _page_specs((KV_WIDTH, PAGE), npg, layer, nchunk)
        + _page_specs((FOX_HEADS, PAGE), npg, layer, nchunk),
        out_specs=pl.BlockSpec((2, t_new, KV_WIDTH), lambda b, c, pt: (0, b, 0)),
        scratch_shapes=[pltpu.VMEM((rows, KV_WIDTH), BF16), pltpu.VMEM((FOX_HEADS, tk), F32),
                        pltpu.VMEM((FOX_HEADS, 1), F32), pltpu.VMEM((rows, LANES), F32)]
        + _decode_scratch(rows, tk),
    )
    return pl.pallas_call(
        functools.partial(_fox_decode_kernel, npg=npg, t_new=t_new, rt=16),
        grid_spec=grid_spec,
        out_shape=jax.ShapeDtypeStruct((2, nreq * t_new, KV_WIDTH), F32),
        compiler_params=_params("parallel", "arbitrary"),
        name="fox_decode",
    )(page_table, q, k_new, v_new, lf_new, *([pool_k_t] * npg), *([pool_v_t] * npg), *([pool_lf_t] * npg))


def _moba_decode_kernel(pt_ref, q_ref, kn_ref, vn_ref, mean_ref, slope_ref, *rest, npg, t_new, past, rt):
    k_refs = rest[:npg]
    v_refs = rest[npg:2 * npg]
    o_ref, qbd_ref, sel_ref, ab_ref, sl_ref = rest[2 * npg:2 * npg + 5]
    scratch = rest[2 * npg + 5:]
    m_ref, l_ref, acc_ref = scratch[-3:]
    c = pl.program_id(1)
    nh = MOBA_HEADS
    rows_all = nh * t_new
    tk = npg * PAGE
    bpc = tk // MOBA_BLOCK
    lpb = MOBA_BLOCK // LANES
    nblk = mean_ref.shape[2]
    hmask = _head_mask(t_new, nh, lambda grp: grp)

    @pl.when(c == 0)
    def _():
        q = q_ref[...]
        qbd = jnp.where(hmask, jnp.concatenate([q] * nh, axis=0), 0.0)
        qbd16 = (qbd * MOBA_SCALE).astype(BF16)
        qbd_ref[...] = qbd16
        sel_ref[...] = _top3_mask(_dot(qbd, mean_ref[0], precision=HI), nblk)
        slope = slope_ref[...]
        t_of_row = _iota((rows_all, 1), 0) % t_new
        ab_ref[...] = jnp.broadcast_to(-slope * (past + t_of_row).astype(F32), ab_ref.shape)
        sl_ref[...] = jnp.broadcast_to(slope, sl_ref.shape)
        bias_new = -slope * (t_of_row - _iota((1, t_new), 1)).astype(F32)
        _new_token_init(qbd16, kn_ref[...], vn_ref[...], bias_new, m_ref, l_ref, acc_ref, t_new)

    keep_bias = {}

    def tile_bias(t, cl):
        rows = slice(t * rt, (t + 1) * rt)
        j = cl // lpb
        if (t, j) not in keep_bias:
            chosen = jnp.sum(jnp.where(_iota((rt, nblk), 1) == c * bpc + j, sel_ref[rows, :], 0.0),
                             axis=-1, keepdims=True)
            keep_bias[(t, j)] = jnp.where(chosen > 0.0, 0.0, NEG)
        kposf = (c * tk + LANES * cl + _iota((1, LANES), 1)).astype(F32)
        return (ab_ref[rows, :] + sl_ref[rows, :] * kposf) + keep_bias[(t, j)]

    _decode_chunk(k_refs, v_refs, qbd_ref, scratch, tile_bias, rows_all, rt)

    @pl.when(c == pl.num_programs(1) - 1)
    def _():
        o = jnp.where(hmask, acc_ref[...] / l_ref[:, 0:1], 0.0)
        tot = jnp.zeros((t_new, KV_WIDTH), F32)
        for h in range(nh):
            tot = tot + o[h * t_new:(h + 1) * t_new, :]
        o_ref[...] = tot


def _moba_decode(q, k_new, v_new, means_t, slopes, pool_k_t, pool_v_t, layer, page_table, t_new):
    nreq, n_pages = page_table.shape
    npg = min(PAGES_PER_STEP, n_pages)
    nchunk = n_pages // npg
    rows = MOBA_HEADS * t_new
    tk = npg * PAGE
    nblk = means_t.shape[2]
    new = lambda w: pl.BlockSpec((t_new, w), lambda b, c, pt: (b, 0))
    grid_spec = pltpu.PrefetchScalarGridSpec(
        num_scalar_prefetch=1,
        grid=(nreq, nchunk),
        in_specs=[new(KV_WIDTH), new(KV_WIDTH), new(KV_WIDTH),
                  pl.BlockSpec((1, KV_WIDTH, nblk), lambda b, c, pt: (b, 0, 0)),
                  pl.BlockSpec((rows, 1), lambda b, c, pt: (0, 0))]
        + _page_specs((KV_WIDTH, PAGE), npg, layer) + _page_specs((KV_WIDTH, PAGE), npg, layer),
        out_specs=pl.BlockSpec((t_new, KV_WIDTH), lambda b, c, pt: (b, 0)),
        scratch_shapes=[pltpu.VMEM((rows, KV_WIDTH), BF16), pltpu.VMEM((rows, nblk), F32),
                        pltpu.VMEM((rows, LANES), F32), pltpu.VMEM((rows, LANES), F32)]
        + _decode_scratch(rows, tk),
    )
    return pl.pallas_call(
        functools.partial(_moba_decode_kernel, npg=npg, t_new=t_new, past=n_pages * PAGE, rt=16),
        grid_spec=grid_spec,
        out_shape=jax.ShapeDtypeStruct((nreq * t_new, KV_WIDTH), F32),
        compiler_params=_params("parallel", "arbitrary"),
        name="moba_decode",
    )(page_table, q, k_new, v_new, means_t, slopes, *([pool_k_t] * npg), *([pool_v_t] * npg))


def _rot_cols(w):
    half = MLA_ROPE // 2
    return jnp.concatenate([-w[..., half:], w[..., :half]], axis=-1)


def _pad_cols(w, width):
    return jnp.pad(w, [(0, 0)] * (w.ndim - 1) + [(0, width - w.shape[-1])])


def _prep_even(w_in, w_uq, w_uk, w_uv):
    o = MLA_Q_RANK + MLA_KV_RANK
    pe = w_in[:, o:o + MLA_ROPE]
    w_lat = w_in[:, :o].astype(BF16)
    w_pe = jnp.concatenate([_pad_cols(pe, 128), _pad_cols(_rot_cols(pe), 128)], axis=1).astype(BF16)
    w_moba = w_in[:, o + MLA_ROPE:].astype(BF16)
    nope = w_uq[:, :, :MLA_NOPE].reshape(MLA_Q_RANK, MLA_HEADS * MLA_NOPE)
    qpe = w_uq[:, :, MLA_NOPE:]
    w_uq_x = jnp.concatenate([nope, _pad_cols(qpe, 128).reshape(MLA_Q_RANK, -1),
                              _pad_cols(_rot_cols(qpe), 128).reshape(MLA_Q_RANK, -1)], axis=1).astype(BF16)
    uk = jnp.transpose(w_uk, (1, 2, 0))
    uv = jnp.transpose(w_uv, (1, 0, 2))
    z_k = jnp.zeros((MLA_NOPE, MLA_KV_RANK), F32)
    z_v = jnp.zeros((MLA_KV_RANK, MLA_V), F32)
    uk_bd = jnp.stack([jnp.block([[uk[2 * p], z_k], [z_k, uk[2 * p + 1]]]) for p in range(MLA_HEADS // 2)])
    uv_bd = jnp.stack([jnp.block([[uv[2 * p], z_v], [z_v, uv[2 * p + 1]]]) for p in range(MLA_HEADS // 2)])
    return (w_lat, w_pe, w_moba), w_uq_x, uk_bd.astype(BF16), uv_bd.astype(BF16)


def _prep_odd(w_in, w_out):
    nq = FOX_HEADS * HEAD_DIM
    nk = FOX_KV_HEADS * HEAD_DIM
    wq = w_in[:, :nq].reshape(D_MODEL, FOX_KV_HEADS, 2, HEAD_DIM).transpose(0, 2, 1, 3).reshape(D_MODEL, nq)
    w_kv = w_in[:, nq:nq + 2 * nk]
    w_f = _pad_cols(w_in[:, nq + 2 * nk:], 128)
    w_out_x = w_out.reshape(FOX_KV_HEADS, 2, HEAD_DIM, D_MODEL).transpose(1, 0, 2, 3).reshape(nq, D_MODEL)
    return (wq.astype(BF16), w_kv.astype(BF16), w_f.astype(BF16)), w_out_x.astype(BF16)


def _rope_tables(pos):
    half = MLA_ROPE // 2
    inv = ROPE_THETA ** (-jnp.arange(half, dtype=F32) / half)
    ang = pos.astype(F32)[:, None] * inv[None, :]
    cos, sin = jnp.cos(ang), jnp.sin(ang)
    return (_pad_cols(jnp.concatenate([cos, cos], axis=-1), 128),
            _pad_cols(jnp.concatenate([sin, sin], axis=-1), 128))


def _row_tile(m):
    for t in (512, 256, 128, 64, 32, 16, 8):
        if m % t == 0:
            return t
    raise ValueError(f"row count {m} is not a multiple of 8")


def _kv_pool_t(pool):
    l, n, pg, h, d = pool.shape
    return jnp.transpose(pool, (0, 1, 3, 4, 2)).reshape(l, n, h * d, pg)


def kernel(x_prompt, x_sample, cache_mla_ckv, cache_mla_kpe, cache_moba_k, cache_moba_v, cache_fox_k, cache_fox_v, cache_fox_logf, page_table, attn_norm, ffn_norm, final_norm, w_in_even, mla_q_norm, mla_kv_norm, mla_w_uq, mla_w_uk, mla_w_uv, w_out_even, w_in_odd, fox_b_forget, w_out_odd, ffn_w_gate, ffn_w_up, ffn_w_down):
    nb, seq, d = x_prompt.shape
    nreq, t_new, _ = x_sample.shape
    n_pages = page_table.shape[1]
    past = n_pages * PAGE
    depth = attn_norm.shape[0]
    assert d == D_MODEL and seq % MOBA_BLOCK == 0 and past % MOBA_BLOCK == 0 and t_new == 8
    assert n_pages % min(PAGES_PER_STEP, n_pages) == 0 and n_pages % min(MLA_PAGES_PER_STEP, n_pages) == 0
    mp, ms = nb * seq, nreq * t_new
    tmp, tms = _row_tile(mp), _row_tile(ms)

    xp = x_prompt.reshape(mp, d)
    xs = x_sample.reshape(ms, d)
    cs_p, sn_p = _rope_tables(jnp.tile(jnp.arange(seq, dtype=jnp.int32), nb))
    cs_s, sn_s = _rope_tables(jnp.tile(past + jnp.arange(t_new, dtype=jnp.int32), nreq))
    slope_h = jnp.exp2(-8.0 * jnp.arange(1, MOBA_HEADS + 1, dtype=F32) / MOBA_HEADS)
    slopes_p = jnp.repeat(slope_h, MOBA_BLOCK).reshape(MOBA_HEADS // 2, 2 * MOBA_BLOCK, 1)
    slopes_s = jnp.repeat(slope_h, t_new).reshape(MOBA_HEADS * t_new, 1)
    pool_moba_k = _kv_pool_t(cache_moba_k)
    pool_moba_v = _kv_pool_t(cache_moba_v)
    pool_fox_k = _kv_pool_t(cache_fox_k)
    pool_fox_v = _kv_pool_t(cache_fox_v)
    pool_fox_lf = jnp.transpose(cache_fox_logf, (0, 1, 3, 2))
    pool_mla_kpe = jnp.transpose(cache_mla_kpe, (0, 1, 3, 2))
    row1 = lambda v: v.reshape(1, -1)

    outs = {k: [] for k in ("ckv_p", "ckv_s", "kpe_p", "kpe_s", "mk_p", "mk_s", "mv_p", "mv_s",
                            "fk_p", "fk_s", "fv_p", "fv_s", "fl_p", "fl_s")}
    for l in range(depth):
        final = l == depth - 1
        ffn = (row1(ffn_norm[l]), ffn_w_gate[l].astype(BF16), ffn_w_up[l].astype(BF16),
               ffn_w_down[l].astype(BF16), row1(final_norm))
        if l % 2 == 0:
            e = l // 2
            w_in_x, w_uq_x, uk_bd, uv_bd = _prep_even(w_in_even[e], mla_w_uq[e], mla_w_uk[e], mla_w_uv[e])
            w_out = w_out_even[e].astype(BF16)
            common = (row1(attn_norm[l]), *w_in_x, row1(mla_q_norm[e]), row1(mla_kv_norm[e]), w_uq_x, uk_bd)
            qcat, kcat, ckv, kpe, mq, mk, mv = _proj_even(xp, *common, cs_p, sn_p, tmp, BF16)
            mix_a = _mla_prompt(qcat, kcat, uv_bd, nb, seq)
            mix_b = _moba_prompt(mq, mk, mv, slopes_p, nb, seq)
            xp = _out_ffn(xp, mix_a, mix_b, w_out, *ffn, tmp, final)
            outs["ckv_p"].append(ckv.reshape(nb, seq, MLA_KV_RANK))
            outs["kpe_p"].append(kpe.reshape(nb, seq, MLA_ROPE))
            outs["mk_p"].append(mk.reshape(nb, seq, MOBA_HEADS, HEAD_DIM))
            outs["mv_p"].append(mv.reshape(nb, seq, MOBA_HEADS, HEAD_DIM))

            qcat, kcat, ckv, kpe, mq, mk, mv = _proj_even(xs, *common, cs_s, sn_s, tms, F32)
            mix_a = _mla_decode(qcat, kcat, uv_bd, cache_mla_ckv, pool_mla_kpe, e, page_table, t_new)
            means_t = _moba_means(pool_moba_k, e, page_table)
            mix_b = _moba_decode(mq, mk, mv, means_t, slopes_s, pool_moba_k, pool_moba_v, e, page_table, t_new)
            xs = _out_ffn(xs, mix_a, mix_b, w_out, *ffn, tms, final)
            outs["ckv_s"].append(ckv.reshape(nreq, t_new, MLA_KV_RANK))
            outs["kpe_s"].append(kpe.reshape(nreq, t_new, MLA_ROPE))
            outs["mk_s"].append(mk.reshape(nreq, t_new, MOBA_HEADS, HEAD_DIM))
            outs["mv_s"].append(mv.reshape(nreq, t_new, MOBA_HEADS, HEAD_DIM))
        else:
            o = l // 2
            w_in_x, w_out = _prep_odd(w_in_odd[o], w_out_odd[o])
            common = (row1(attn_norm[l]), *w_in_x, row1(fox_b_forget[o]))
            q, k, v, lf = _proj_odd(xp, *common, tmp, BF16)
            f, ft = _fox_cumsum(lf, lf.reshape(nb, seq, FOX_HEADS).transpose(0, 2, 1), nb, seq)
            fq = f.reshape(mp, 4, 4).transpose(1, 0, 2)
            mix = _fox_prompt(q, k, v, fq, ft.reshape(nb, 4, 4, seq), nb, seq)
            xp = _out_ffn(xp, mix[0], mix[1], w_out, *ffn, tmp, final)
            outs["fk_p"].append(k.reshape(nb, seq, FOX_KV_HEADS, HEAD_DIM))
            outs["fv_p"].append(v.reshape(nb, seq, FOX_KV_HEADS, HEAD_DIM))
            outs["fl_p"].append(lf.reshape(nb, seq, FOX_HEADS))

            q, k, v, lf = _proj_odd(xs, *common, tms, F32)
            mix = _fox_decode(q, k, v, lf, pool_fox_k, pool_fox_v, pool_fox_lf, o, page_table, t_new)
            xs = _out_ffn(xs, mix[0], mix[1], w_out, *ffn, tms, final)
            outs["fk_s"].append(k.reshape(nreq, t_new, FOX_KV_HEADS, HEAD_DIM))
            outs["fv_s"].append(v.reshape(nreq, t_new, FOX_KV_HEADS, HEAD_DIM))
            outs["fl_s"].append(lf.reshape(nreq, t_new, FOX_HEADS))

    st = lambda name: jnp.stack(outs[name])
    return (xp.reshape(nb, seq, d), xs.reshape(nreq, t_new, d),
            st("ckv_p"), st("ckv_s"), st("kpe_p"), st("kpe_s"), st("mk_p"), st("mk_s"), st("mv_p"), st("mv_s"),
            st("fk_p"), st("fk_s"), st("fv_p"), st("fv_s"), st("fl_p"), st("fl_s"))
```

```python
import functools

import jax
import jax.numpy as jnp
from jax import lax
from jax.experimental import pallas as pl
from jax.experimental.pallas import tpu as pltpu

F32 = jnp.float32
BF16 = jnp.bfloat16

D_MODEL = 1024
HEAD_DIM = 64
PAGE = 128
RMS_EPS = 1e-6
MLA_HEADS = 8
MLA_Q_RANK = 256
MLA_KV_RANK = 128
MLA_NOPE = 64
MLA_ROPE = 32
MLA_V = 64
ROPE_THETA = 10000.0
MLA_SCALE = (MLA_NOPE + MLA_ROPE) ** -0.5
MOBA_HEADS = 8
MOBA_BLOCK = 256
MOBA_TOPK = 3
MOBA_SCALE = HEAD_DIM ** -0.5
FOX_HEADS = 16
FOX_KV_HEADS = 8
FOX_SCALE = HEAD_DIM ** -0.5
FFN_HIDDEN = 2816
FFN_CHUNK = 1408
LANES = 128
KV_WIDTH = 512
NEG = -1e30
VMEM_LIMIT = 56 * 1024 * 1024
PAGES_PER_STEP = 32
MLA_PAGES_PER_STEP = 64

HI = lax.Precision.HIGHEST


def _nt(a, b, precision=None):
    return lax.dot_general(a, b, (((1,), (1,)), ((), ())), precision=precision,
                           preferred_element_type=F32)


def _dot(a, b, precision=None):
    return jnp.dot(a, b, precision=precision, preferred_element_type=F32)


def _rms(x, g):
    return x * lax.rsqrt(jnp.mean(x * x, axis=-1, keepdims=True) + RMS_EPS) * g


def _params(*sem):
    return pltpu.CompilerParams(dimension_semantics=sem, vmem_limit_bytes=VMEM_LIMIT)


def _const_spec(shape):
    nd = len(shape)
    return pl.BlockSpec(shape, lambda *a: (0,) * nd, pipeline_mode=pl.Buffered(1))


def _iota(shape, dim):
    return lax.broadcasted_iota(jnp.int32, shape, dim)


def _proj_even_kernel(x_ref, g_ref, wlat_ref, wpe_ref, wmoba_ref, qn_ref, kvn_ref, wuq_ref, wuk_ref, cs_ref, sn_ref,
                      qcat_ref, kcat_ref, ckv_ref, kpe_ref, mq_ref, mk_ref, mv_ref):
    h = _rms(x_ref[...], g_ref[...]).astype(BF16)
    cq = _rms(_dot(h, wlat_ref[:, 0:256]), qn_ref[...]).astype(BF16)
    ckv = _rms(_dot(h, wlat_ref[:, 256:384]), kvn_ref[...])
    cs = cs_ref[...]
    sn = sn_ref[...]
    kpe = _dot(h, wpe_ref[:, 0:128]) * cs + _dot(h, wpe_ref[:, 128:256]) * sn
    ckv_ref[...] = ckv
    kpe_ref[...] = kpe[:, :MLA_ROPE]
    kcat_ref[:, 0:128] = ckv.astype(kcat_ref.dtype)
    kcat_ref[:, 128:256] = kpe.astype(kcat_ref.dtype)
    mq_ref[...] = _dot(h, wmoba_ref[:, 0:512])
    mk_ref[...] = _dot(h, wmoba_ref[:, 512:1024])
    mv_ref[...] = _dot(h, wmoba_ref[:, 1024:1536])
    qn = _dot(cq, wuq_ref[:, 0:512]).astype(BF16)
    for p in range(MLA_HEADS // 2):
        ql = _dot(qn[:, 128 * p:128 * p + 128], wuk_ref[p])
        for hh in range(2):
            hd = 2 * p + hh
            qpe = (_dot(cq, wuq_ref[:, 512 + 128 * hd:640 + 128 * hd]) * cs
                   + _dot(cq, wuq_ref[:, 1536 + 128 * hd:1664 + 128 * hd]) * sn)
            qcat_ref[:, 256 * hd:256 * hd + 128] = (ql[:, 128 * hh:128 * hh + 128] * MLA_SCALE).astype(qcat_ref.dtype)
            qcat_ref[:, 256 * hd + 128:256 * hd + 256] = (qpe * MLA_SCALE).astype(qcat_ref.dtype)


def _proj_even(x, g, w_lat, w_pe, w_moba, qn, kvn, w_uq, w_uk, cs, sn, tm, act_dtype):
    m = x.shape[0]
    row = lambda w: pl.BlockSpec((tm, w), lambda i: (i, 0))
    out_shape = (
        jax.ShapeDtypeStruct((m, 2048), act_dtype),
        jax.ShapeDtypeStruct((m, 256), act_dtype),
        jax.ShapeDtypeStruct((m, 128), F32),
        jax.ShapeDtypeStruct((m, 32), F32),
        jax.ShapeDtypeStruct((m, 512), F32),
        jax.ShapeDtypeStruct((m, 512), F32),
        jax.ShapeDtypeStruct((m, 512), F32),
    )
    return pl.pallas_call(
        _proj_even_kernel,
        grid=(m // tm,),
        in_specs=[row(D_MODEL), _const_spec((1, D_MODEL)), _const_spec(w_lat.shape), _const_spec(w_pe.shape),
                  _const_spec(w_moba.shape), _const_spec((1, 256)), _const_spec((1, 128)),
                  _const_spec(w_uq.shape), _const_spec(w_uk.shape), row(128), row(128)],
        out_specs=(row(2048), row(256), row(128), row(32), row(512), row(512), row(512)),
        out_shape=out_shape,
        compiler_params=_params("parallel"),
        name="proj_even",
    )(x, g, w_lat, w_pe, w_moba, qn, kvn, w_uq, w_uk, cs, sn)


def _proj_odd_kernel(x_ref, g_ref, wq_ref, wkv_ref, wf_ref, bf_ref, q_ref, k_ref, v_ref, lf_ref):
    h = _rms(x_ref[...], g_ref[...]).astype(BF16)
    q_ref[...] = _dot(h, wq_ref[...]).astype(q_ref.dtype)
    k_ref[...] = _dot(h, wkv_ref[:, 0:512])
    v_ref[...] = _dot(h, wkv_ref[:, 512:1024])
    zf = _dot(h, wf_ref[...])[:, :FOX_HEADS] + bf_ref[...]
    lf_ref[...] = -(jnp.maximum(-zf, 0.0) + jnp.log1p(jnp.exp(-jnp.abs(zf))))


def _proj_odd(x, g, w_q, w_kv, w_f, b_f, tm, act_dtype):
    m = x.shape[0]
    row = lambda w: pl.BlockSpec((tm, w), lambda i: (i, 0))
    out_shape = (
        jax.ShapeDtypeStruct((m, 1024), act_dtype),
        jax.ShapeDtypeStruct((m, 512), F32),
        jax.ShapeDtypeStruct((m, 512), F32),
        jax.ShapeDtypeStruct((m, FOX_HEADS), F32),
    )
    return pl.pallas_call(
        _proj_odd_kernel,
        grid=(m // tm,),
        in_specs=[row(D_MODEL), _const_spec((1, D_MODEL)), _const_spec(w_q.shape), _const_spec(w_kv.shape),
                  _const_spec(w_f.shape), _const_spec((1, FOX_HEADS))],
        out_specs=(row(1024), row(512), row(512), row(FOX_HEADS)),
        out_shape=out_shape,
        compiler_params=_params("parallel"),
        name="proj_odd",
    )(x, g, w_q, w_kv, w_f, b_f)


def _out_ffn_kernel(x_ref, a_ref, b_ref, wo_ref, g_ref, wg_ref, wu_ref, wd_ref, fn_ref, o_ref, *, final):
    x1 = (x_ref[...] + _dot(a_ref[...].astype(BF16), wo_ref[0:512, :])
          + _dot(b_ref[...].astype(BF16), wo_ref[512:1024, :]))
    h = _rms(x1, g_ref[...]).astype(BF16)
    acc = jnp.zeros_like(x1)
    for c in range(FFN_HIDDEN // FFN_CHUNK):
        lo = c * FFN_CHUNK
        gate = _dot(h, wg_ref[:, lo:lo + FFN_CHUNK])
        up = _dot(h, wu_ref[:, lo:lo + FFN_CHUNK])
        act = (gate / (1.0 + jnp.exp(-gate))) * up
        acc = acc + _dot(act.astype(BF16), wd_ref[lo:lo + FFN_CHUNK, :])
    out = x1 + acc
    if final:
        out = _rms(out, fn_ref[...])
    o_ref[...] = out


def _out_ffn(x, mix_a, mix_b, w_out, g, w_gate, w_up, w_down, final_norm, tm, final):
    m = x.shape[0]
    row = lambda w: pl.BlockSpec((tm, w), lambda i: (i, 0))
    return pl.pallas_call(
        functools.partial(_out_ffn_kernel, final=final),
        grid=(m // tm,),
        in_specs=[row(D_MODEL), row(512), row(512), _const_spec(w_out.shape), _const_spec((1, D_MODEL)),
                  _const_spec(w_gate.shape), _const_spec(w_up.shape), _const_spec(w_down.shape),
                  _const_spec((1, D_MODEL))],
        out_specs=row(D_MODEL),
        out_shape=jax.ShapeDtypeStruct((m, D_MODEL), F32),
        compiler_params=_params("parallel"),
        name="out_ffn",
    )(x, mix_a, mix_b, w_out, g, w_gate, w_up, w_down, final_norm)


def _init_stats(m_ref, l_ref, acc_ref):
    m_ref[...] = jnp.full(m_ref.shape, NEG, F32)
    l_ref[...] = jnp.zeros(l_ref.shape, F32)
    acc_ref[...] = jnp.zeros(acc_ref.shape, F32)


def _softmax_tile(tiles, rows, m_ref, l_ref, p_ref, acc_ref, alpha_ref=None):
    mx = tiles[0]
    for t in tiles[1:]:
        mx = jnp.maximum(mx, t)
    m_old = m_ref[rows, :]
    m_new = jnp.maximum(m_old, jnp.max(mx, axis=-1, keepdims=True))
    alpha = jnp.exp(m_old - m_new)
    ps = None
    for c, t in enumerate(tiles):
        p = jnp.exp(t - m_new)
        ps = p if ps is None else ps + p
        p_ref[rows, LANES * c:LANES * (c + 1)] = p.astype(p_ref.dtype)
    l_ref[rows, :] = alpha * l_ref[rows, :] + jnp.sum(ps, axis=-1, keepdims=True)
    m_ref[rows, :] = m_new
    if alpha_ref is not None:
        alpha_ref[rows, :] = alpha
        return
    for c in range(acc_ref.shape[1] // LANES):
        acc_ref[rows, LANES * c:LANES * (c + 1)] = acc_ref[rows, LANES * c:LANES * (c + 1)] * alpha


def _causal_blocks(n_plain, qk, softmax, pv):
    qk(0, 0)

    def pair(jj, carry):
        j = 2 * jj
        qk(j + 1, 1)
        softmax(j, 0, False)
        pv(j, 0)
        qk(j + 2, 0)
        softmax(j + 1, 1, False)
        pv(j + 1, 1)
        return carry

    lax.fori_loop(0, n_plain // 2, pair, 0)

    @pl.when(n_plain % 2 == 0)
    def _():
        softmax(n_plain, 0, True)
        pv(n_plain, 0)

    @pl.when(n_plain % 2 == 1)
    def _():
        qk(n_plain, 1)
        softmax(n_plain - 1, 0, False)
        pv(n_plain - 1, 0)
        softmax(n_plain, 1, True)
        pv(n_plain, 1)


def _flash_scratch(rows, tk, width):
    return [pltpu.VMEM((rows, tk), F32), pltpu.VMEM((rows, tk), F32),
            pltpu.VMEM((rows, tk), BF16), pltpu.VMEM((rows, tk), BF16),
            pltpu.VMEM((rows, LANES), F32), pltpu.VMEM((rows, LANES), F32),
            pltpu.VMEM((rows, LANES), F32), pltpu.VMEM((rows, LANES), F32), pltpu.VMEM((rows, width), F32)]


def _mla_prompt_kernel(q_ref, k_ref, wuv_ref, o_ref, qs_ref, s0_ref, s1_ref, p0_ref, p1_ref, a0_ref, a1_ref,
                       m_ref, l_ref, acc_ref, *, tq, tk, rt):
    i = pl.program_id(1)
    nh = MLA_HEADS
    nl = tk // LANES
    s_refs, p_refs, a_refs = (s0_ref, s1_ref), (p0_ref, p1_ref), (a0_ref, a1_ref)
    for h in range(nh):
        qs_ref[h * tq:(h + 1) * tq, :] = q_ref[:, 256 * h:256 * h + 256]
    _init_stats(m_ref, l_ref, acc_ref)
    j_last = (i * tq + tq - 1) // tk

    def qk(j, buf):
        kb = k_ref[pl.ds(pl.multiple_of(j * tk, tk), tk), :]
        s_refs[buf][...] = _nt(qs_ref[...], kb)

    def pv(j, buf):
        vb = k_ref[pl.ds(pl.multiple_of(j * tk, tk), tk), 0:MLA_KV_RANK]
        acc_ref[...] = acc_ref[...] * a_refs[buf][...] + _dot(p_refs[buf][...], vb)

    def softmax(j, buf, masked):
        for t in range(nh * tq // rt):
            rows = slice(t * rt, (t + 1) * rt)
            tiles = []
            for c in range(nl):
                s = s_refs[buf][rows, LANES * c:LANES * (c + 1)]
                if masked:
                    qpos = i * tq + (t * rt) % tq + _iota((rt, 1), 0)
                    kpos = j * tk + LANES * c + _iota((1, LANES), 1)
                    s = jnp.where(kpos <= qpos, s, NEG)
                tiles.append(s)
            _softmax_tile(tiles, rows, m_ref, l_ref, p_refs[buf], acc_ref, a_refs[buf])

    _causal_blocks(j_last, qk, softmax, pv)
    o = acc_ref[...] / l_ref[...]
    for p in range(nh // 2):
        pair = jnp.concatenate([o[(2 * p) * tq:(2 * p + 1) * tq], o[(2 * p + 1) * tq:(2 * p + 2) * tq]], axis=1)
        o_ref[:, 128 * p:128 * p + 128] = _dot(pair.astype(BF16), wuv_ref[p]).astype(o_ref.dtype)


def _mla_prompt(qcat, kcat, w_uv_bd, nb, seq):
    tq, tk, rt = 128, 256, 32
    nq = seq // tq
    rows = MLA_HEADS * tq
    return pl.pallas_call(
        functools.partial(_mla_prompt_kernel, tq=tq, tk=tk, rt=rt),
        grid=(nb, nq),
        in_specs=[pl.BlockSpec((tq, 2048), lambda b, i: (b * nq + i, 0)),
                  pl.BlockSpec((seq, 256), lambda b, i: (b, 0)),
                  _const_spec(w_uv_bd.shape)],
        out_specs=pl.BlockSpec((tq, 512), lambda b, i: (b * nq + i, 0)),
        out_shape=jax.ShapeDtypeStruct((nb * seq, 512), BF16),
        scratch_shapes=[pltpu.VMEM((rows, 256), BF16)] + _flash_scratch(rows, tk, MLA_KV_RANK),
        compiler_params=_params("parallel", "arbitrary"),
        name="mla_prompt",
    )(qcat, kcat, w_uv_bd)


def _top3_mask(scores, n_valid):
    rows, nblk = scores.shape
    lane = _iota((rows, nblk), 1).astype(F32)
    avail = jnp.where(lane < n_valid, 1.0, 0.0)
    sel = jnp.zeros((rows, nblk), F32)
    for _ in range(MOBA_TOPK):
        work = jnp.where(avail > 0.0, scores, -jnp.inf)
        mx = jnp.max(work, axis=-1, keepdims=True)
        cand = jnp.where(work == mx, jnp.where(avail > 0.0, lane, float(nblk)), float(nblk))
        idx = jnp.min(cand, axis=-1, keepdims=True)
        pick = lane == idx
        sel = jnp.where(pick, 1.0, sel)
        avail = jnp.where(pick, 0.0, avail)
    return sel


def _moba_prompt_kernel(q_ref, k_ref, v_ref, slope_ref, o_ref, mean_ref, qbd_ref, sel_ref, ab_ref,
                        s0_ref, s1_ref, p0_ref, p1_ref, a0_ref, a1_ref, m_ref, l_ref, acc_ref, *, tq, nblk, rt):
    i = pl.program_id(2)
    blk = MOBA_BLOCK
    nl = blk // LANES
    rows_all = 2 * tq
    s_refs, p_refs, a_refs = (s0_ref, s1_ref), (p0_ref, p1_ref), (a0_ref, a1_ref)

    @pl.when(i == 0)
    def _():
        for j in range(nblk):
            mean_ref[j:j + 1, :] = jnp.sum(k_ref[j * blk:(j + 1) * blk, :], axis=0, keepdims=True) * (1.0 / blk)

    q = q_ref[...]
    lane = _iota((tq, LANES), 1)
    qbd = jnp.concatenate([jnp.where(lane < HEAD_DIM, q, 0.0), jnp.where(lane >= HEAD_DIM, q, 0.0)], axis=0)
    qbd_ref[...] = (qbd * MOBA_SCALE).astype(BF16)
    sel_ref[...] = _top3_mask(_nt(qbd, mean_ref[...], precision=HI), i)
    qposf = (i * tq + _iota((rows_all, 1), 0) % tq).astype(F32)
    ab_ref[...] = jnp.broadcast_to(-slope_ref[0] * qposf, (rows_all, LANES))
    _init_stats(m_ref, l_ref, acc_ref)

    def qk(j, buf):
        kb = k_ref[pl.ds(pl.multiple_of(j * blk, blk), blk), :].astype(BF16)
        s_refs[buf][...] = _nt(qbd_ref[...], kb)

    def pv(j, buf):
        vb = v_ref[pl.ds(pl.multiple_of(j * blk, blk), blk), :].astype(BF16)
        acc_ref[...] = acc_ref[...] * a_refs[buf][...] + _dot(p_refs[buf][...], vb)

    def softmax(j, buf, own):
        for g in range(2):
            sg = slope_ref[0, g * tq:g * tq + 1, :]
            cols = [sg * (j * blk + LANES * c + _iota((1, LANES), 1)).astype(F32) for c in range(nl)]
            for t in range(tq // rt):
                r0 = g * tq + t * rt
                rows = slice(r0, r0 + rt)
                rb = ab_ref[rows, :]
                if not own:
                    chosen = jnp.sum(jnp.where(_iota((rt, nblk), 1) == j, sel_ref[rows, :], 0.0),
                                     axis=-1, keepdims=True)
                    rb = rb + jnp.where(chosen > 0.0, 0.0, NEG)
                tiles = []
                for c in range(nl):
                    s = s_refs[buf][rows, LANES * c:LANES * (c + 1)] + (rb + cols[c])
                    if own:
                        s = jnp.where(LANES * c + _iota((1, LANES), 1) <= t * rt + _iota((rt, 1), 0), s, NEG)
                    tiles.append(s)
                _softmax_tile(tiles, rows, m_ref, l_ref, p_refs[buf], acc_ref, a_refs[buf])

    qk(i, 0)
    qk(0, 1)
    softmax(i, 0, True)
    pv(i, 0)

    def pair(jj, carry):
        j = 2 * jj
        qk(j + 1, 0)
        softmax(j, 1, False)
        pv(j, 1)
        qk(j + 2, 1)
        softmax(j + 1, 0, False)
        pv(j + 1, 0)
        return carry

    lax.fori_loop(0, i // 2, pair, 0)

    @pl.when(i % 2 == 1)
    def _():
        softmax(i - 1, 1, False)
        pv(i - 1, 1)

    o = acc_ref[...] / l_ref[...]
    o_ref[...] = jnp.where(lane < HEAD_DIM, o[0:tq], o[tq:2 * tq]).astype(o_ref.dtype)


def _moba_prompt(mq, mk, mv, slopes, nb, seq):
    tq, rt = MOBA_BLOCK, 32
    nq = seq // tq
    nblk = seq // MOBA_BLOCK
    npair = MOBA_HEADS // 2
    rows = 2 * tq
    return pl.pallas_call(
        functools.partial(_moba_prompt_kernel, tq=tq, nblk=nblk, rt=rt),
        grid=(nb, npair, nq),
        in_specs=[pl.BlockSpec((tq, LANES), lambda b, p, i: (b * nq + i, p)),
                  pl.BlockSpec((seq, LANES), lambda b, p, i: (b, p)),
                  pl.BlockSpec((seq, LANES), lambda b, p, i: (b, p)),
                  pl.BlockSpec((1, rows, 1), lambda b, p, i: (p, 0, 0))],
        out_specs=pl.BlockSpec((tq, LANES), lambda b, p, i: (b * nq + i, p)),
        out_shape=jax.ShapeDtypeStruct((nb * seq, 512), BF16),
        scratch_shapes=[pltpu.VMEM((nblk, LANES), F32), pltpu.VMEM((rows, LANES), BF16),
                        pltpu.VMEM((rows, nblk), F32), pltpu.VMEM((rows, LANES), F32)]
        + _flash_scratch(rows, MOBA_BLOCK, LANES),
        compiler_params=_params("parallel", "parallel", "arbitrary"),
        name="moba_prompt",
    )(mq, mk, mv, slopes)


def _fox_cumsum_kernel(lf_ref, lft_ref, f_ref, ft_ref, *, seq, ch):
    r = _iota((ch, ch), 0)
    c = _iota((ch, ch), 1)
    lower = (c <= r).astype(F32)
    upper = (r <= c).astype(F32)
    carry = jnp.zeros((1, FOX_HEADS), F32)
    carry_t = jnp.zeros((FOX_HEADS, 1), F32)
    for j in range(seq // ch):
        f = _dot(lower, lf_ref[j * ch:(j + 1) * ch, :], precision=HI) + carry
        f_ref[j * ch:(j + 1) * ch, :] = f
        carry = f[ch - 1:ch, :]
        ft = _dot(lft_ref[0, :, j * ch:(j + 1) * ch], upper, precision=HI) + carry_t
        ft_ref[0, :, j * ch:(j + 1) * ch] = ft
        carry_t = ft[:, ch - 1:ch]


def _fox_cumsum(logf, logf_t, nb, seq):
    return pl.pallas_call(
        functools.partial(_fox_cumsum_kernel, seq=seq, ch=256),
        grid=(nb,),
        in_specs=[pl.BlockSpec((seq, FOX_HEADS), lambda b: (b, 0)),
                  pl.BlockSpec((1, FOX_HEADS, seq), lambda b: (b, 0, 0))],
        out_specs=(pl.BlockSpec((seq, FOX_HEADS), lambda b: (b, 0)),
                   pl.BlockSpec((1, FOX_HEADS, seq), lambda b: (b, 0, 0))),
        out_shape=(jax.ShapeDtypeStruct((nb * seq, FOX_HEADS), F32),
                   jax.ShapeDtypeStruct((nb, FOX_HEADS, seq), F32)),
        compiler_params=_params("parallel"),
        name="fox_cumsum",
    )(logf, logf_t)


def _fox_prompt_kernel(q0_ref, q1_ref, k_ref, v_ref, fq_ref, fk_ref, o_ref, qbd_ref, fqr_ref,
                       s0_ref, s1_ref, p0_ref, p1_ref, a0_ref, a1_ref, m_ref, l_ref, acc_ref, *, tq, rt):
    i = pl.program_id(2)
    tk = tq
    nl = tk // LANES
    s_refs, p_refs, a_refs = (s0_ref, s1_ref), (p0_ref, p1_ref), (a0_ref, a1_ref)
    lane = _iota((tq, LANES), 1)
    lo = lane < HEAD_DIM
    q0 = q0_ref[...] * FOX_SCALE
    q1 = q1_ref[...] * FOX_SCALE
    zero = jnp.zeros_like(q0)
    for r, blkq in enumerate((jnp.where(lo, q0, zero), jnp.where(lo, q1, zero),
                              jnp.where(lo, zero, q0), jnp.where(lo, zero, q1))):
        qbd_ref[r * tq:(r + 1) * tq, :] = blkq
        fqr_ref[r * tq:(r + 1) * tq, :] = jnp.broadcast_to(fq_ref[0, :, r:r + 1], (tq, LANES))
    _init_stats(m_ref, l_ref, acc_ref)

    def qk(j, buf):
        kb = k_ref[pl.ds(pl.multiple_of(j * tk, tk), tk), :].astype(BF16)
        s_refs[buf][...] = _nt(qbd_ref[...], kb)

    def pv(j, buf):
        vb = v_ref[pl.ds(pl.multiple_of(j * tk, tk), tk), :].astype(BF16)
        acc_ref[...] = acc_ref[...] * a_refs[buf][...] + _dot(p_refs[buf][...], vb)

    def softmax(j, buf, masked):
        for r in range(4):
            fk = [fk_ref[0, 0, r:r + 1, pl.ds(pl.multiple_of(j * tk + LANES * c, LANES), LANES)] for c in range(nl)]
            for t in range(tq // rt):
                r0 = r * tq + t * rt
                rows = slice(r0, r0 + rt)
                fq = fqr_ref[rows, :]
                tiles = []
                for c in range(nl):
                    s = s_refs[buf][rows, LANES * c:LANES * (c + 1)] + (fq - fk[c])
                    if masked:
                        s = jnp.where(LANES * c + _iota((1, LANES), 1) <= t * rt + _iota((rt, 1), 0), s, NEG)
                    tiles.append(s)
                _softmax_tile(tiles, rows, m_ref, l_ref, p_refs[buf], acc_ref, a_refs[buf])

    _causal_blocks(i, qk, softmax, pv)
    o = acc_ref[...] / l_ref[...]
    o_ref[0] = jnp.where(lo, o[0:tq], o[2 * tq:3 * tq]).astype(o_ref.dtype)
    o_ref[1] = jnp.where(lo, o[tq:2 * tq], o[3 * tq:4 * tq]).astype(o_ref.dtype)


def _fox_prompt(q, k, v, fq, fk, nb, seq):
    tq, rt = 256, 32
    nq = seq // tq
    npair = FOX_KV_HEADS // 2
    rows = 4 * tq
    return pl.pallas_call(
        functools.partial(_fox_prompt_kernel, tq=tq, rt=rt),
        grid=(nb, npair, nq),
        in_specs=[pl.BlockSpec((tq, LANES), lambda b, p, i: (b * nq + i, p)),
                  pl.BlockSpec((tq, LANES), lambda b, p, i: (b * nq + i, npair + p)),
                  pl.BlockSpec((seq, LANES), lambda b, p, i: (b, p)),
                  pl.BlockSpec((seq, LANES), lambda b, p, i: (b, p)),
                  pl.BlockSpec((1, tq, 4), lambda b, p, i: (p, b * nq + i, 0)),
                  pl.BlockSpec((1, 1, 4, seq), lambda b, p, i: (b, p, 0, 0))],
        out_specs=pl.BlockSpec((2, tq, LANES), lambda b, p, i: (0, b * nq + i, p)),
        out_shape=jax.ShapeDtypeStruct((2, nb * seq, 512), BF16),
        scratch_shapes=[pltpu.VMEM((rows, LANES), BF16), pltpu.VMEM((rows, LANES), F32)]
        + _flash_scratch(rows, tq, LANES),
        compiler_params=_params("parallel", "parallel", "arbitrary"),
        name="fox_prompt",
    )(q, q, k, v, fq, fk)


def _page_specs(block, npg, layer, reverse_chunks=None):
    specs = []
    for pg in range(npg):
        if reverse_chunks is None:
            imap = lambda b, c, pt, pg=pg: (layer, pt[b, c * npg + pg], 0, 0)
        else:
            imap = lambda b, c, pt, pg=pg: (layer, pt[b, (reverse_chunks - 1 - c) * npg + pg], 0, 0)
        specs.append(pl.BlockSpec((1, 1) + block, imap))
    return specs


def _new_token_init(qbd16, kn, vn, bias_new, m_ref, l_ref, acc_ref, t_new):
    rows = qbd16.shape[0]
    s = _nt(qbd16, kn.astype(BF16))
    if bias_new is not None:
        s = s + bias_new
    tq = _iota((rows, t_new), 0) % t_new
    tk = _iota((rows, t_new), 1)
    s = jnp.where(tk <= tq, s, NEG)
    m = jnp.max(s, axis=-1, keepdims=True)
    p = jnp.exp(s - m)
    acc = jnp.zeros((rows, vn.shape[1]), F32)
    for t in range(t_new):
        acc = acc + p[:, t:t + 1] * vn[t:t + 1, :]
    m_ref[...] = jnp.broadcast_to(m, m_ref.shape)
    l_ref[...] = jnp.broadcast_to(jnp.sum(p, axis=-1, keepdims=True), l_ref.shape)
    acc_ref[...] = acc


def _mla_decode_kernel(pt_ref, q_ref, kn_ref, wuv_ref, *rest, npg, t_new, rt):
    ckv_refs = rest[:npg]
    kpe_refs = rest[npg:2 * npg]
    o_ref, qs_ref, ck_ref, kpt_ref, s_ref, p_ref, m_ref, l_ref, acc_ref = rest[2 * npg:]
    c = pl.program_id(1)
    nh = MLA_HEADS
    rows_all = nh * t_new
    tk = npg * PAGE

    @pl.when(c == 0)
    def _():
        q = q_ref[...]
        qs = jnp.concatenate([q[:, 256 * h:256 * h + 256] for h in range(nh)], axis=0).astype(BF16)
        qs_ref[...] = qs
        kn = kn_ref[...]
        _new_token_init(qs, kn, kn[:, 0:MLA_KV_RANK], None, m_ref, l_ref, acc_ref, t_new)

    for pg in range(npg):
        ck_ref[PAGE * pg:PAGE * (pg + 1), :] = ckv_refs[pg][0, 0].astype(BF16)
        kpt_ref[:, PAGE * pg:PAGE * (pg + 1)] = kpe_refs[pg][0, 0].astype(BF16)
    s_ref[...] = (_nt(qs_ref[:, 0:MLA_KV_RANK], ck_ref[...])
                  + _dot(qs_ref[:, MLA_KV_RANK:MLA_KV_RANK + MLA_ROPE], kpt_ref[...]))
    for t in range(rows_all // rt):
        rows = slice(t * rt, (t + 1) * rt)
        tiles = [s_ref[rows, LANES * cl:LANES * (cl + 1)] for cl in range(tk // LANES)]
        _softmax_tile(tiles, rows, m_ref, l_ref, p_ref, acc_ref)
    acc_ref[...] += _dot(p_ref[...], ck_ref[...])

    @pl.when(c == pl.num_programs(1) - 1)
    def _():
        o = acc_ref[...] / l_ref[...]
        for p in range(nh // 2):
            pair = jnp.concatenate([o[(2 * p) * t_new:(2 * p + 1) * t_new],
                                    o[(2 * p + 1) * t_new:(2 * p + 2) * t_new]], axis=1)
            o_ref[:, 128 * p:128 * p + 128] = _dot(pair.astype(BF16), wuv_ref[p]).astype(o_ref.dtype)


def _mla_decode(qcat, kcat, w_uv_bd, pool_ckv, pool_kpe_t, layer, page_table, t_new):
    nreq, n_pages = page_table.shape
    npg = min(MLA_PAGES_PER_STEP, n_pages)
    nchunk = n_pages // npg
    rows = MLA_HEADS * t_new
    tk = npg * PAGE
    grid_spec = pltpu.PrefetchScalarGridSpec(
        num_scalar_prefetch=1,
        grid=(nreq, nchunk),
        in_specs=[pl.BlockSpec((t_new, 2048), lambda b, c, pt: (b, 0)),
                  pl.BlockSpec((t_new, 256), lambda b, c, pt: (b, 0)),
                  pl.BlockSpec(w_uv_bd.shape, lambda b, c, pt: (0, 0, 0))]
        + _page_specs((PAGE, MLA_KV_RANK), npg, layer) + _page_specs((MLA_ROPE, PAGE), npg, layer),
        out_specs=pl.BlockSpec((t_new, 512), lambda b, c, pt: (b, 0)),
        scratch_shapes=[pltpu.VMEM((rows, 256), BF16), pltpu.VMEM((tk, MLA_KV_RANK), BF16),
                        pltpu.VMEM((MLA_ROPE, tk), BF16), pltpu.VMEM((rows, tk), F32), pltpu.VMEM((rows, tk), BF16),
                        pltpu.VMEM((rows, LANES), F32), pltpu.VMEM((rows, LANES), F32),
                        pltpu.VMEM((rows, MLA_KV_RANK), F32)],
    )
    return pl.pallas_call(
        functools.partial(_mla_decode_kernel, npg=npg, t_new=t_new, rt=16),
        grid_spec=grid_spec,
        out_shape=jax.ShapeDtypeStruct((nreq * t_new, 512), F32),
        compiler_params=_params("parallel", "arbitrary"),
        name="mla_decode",
    )(page_table, qcat, kcat, w_uv_bd, *([pool_ckv] * npg), *([pool_kpe_t] * npg))


def _moba_means_kernel(pt_ref, *rest, npg):
    k_refs = rest[:npg]
    o_ref = rest[npg]
    c = pl.program_id(1)
    ppb = MOBA_BLOCK // PAGE
    bpc = npg // ppb
    nblk = o_ref.shape[2]

    @pl.when(c == 0)
    def _():
        o_ref[...] = jnp.zeros(o_ref.shape, F32)

    lane = _iota((KV_WIDTH, nblk), 1)
    acc = o_ref[0]
    for j in range(bpc):
        tot = k_refs[ppb * j][0, 0]
        for pg in range(1, ppb):
            tot = tot + k_refs[ppb * j + pg][0, 0]
        col = jnp.sum(tot, axis=1, keepdims=True) * (1.0 / MOBA_BLOCK)
        acc = jnp.where(lane == c * bpc + j, col, acc)
    o_ref[0] = acc


def _moba_means(pool_k_t, layer, page_table):
    nreq, n_pages = page_table.shape
    npg = min(PAGES_PER_STEP, n_pages)
    nchunk = n_pages // npg
    nblk = n_pages * PAGE // MOBA_BLOCK
    grid_spec = pltpu.PrefetchScalarGridSpec(
        num_scalar_prefetch=1,
        grid=(nreq, nchunk),
        in_specs=_page_specs((KV_WIDTH, PAGE), npg, layer),
        out_specs=pl.BlockSpec((1, KV_WIDTH, nblk), lambda b, c, pt: (b, 0, 0)),
    )
    return pl.pallas_call(
        functools.partial(_moba_means_kernel, npg=npg),
        grid_spec=grid_spec,
        out_shape=jax.ShapeDtypeStruct((nreq, KV_WIDTH, nblk), F32),
        compiler_params=_params("parallel", "arbitrary"),
        name="moba_means",
    )(page_table, *([pool_k_t] * npg))


def _head_mask(rows_per_group, n_groups, head_of_group):
    rows = n_groups * rows_per_group
    grp = _iota((rows, KV_WIDTH), 0) // rows_per_group
    blk = _iota((rows, KV_WIDTH), 1) // HEAD_DIM
    return blk == head_of_group(grp)


DECODE_PARTS = 2


def _decode_scratch(rows, tk):
    tkp = tk // DECODE_PARTS
    per_part = lambda shape, dt: [pltpu.VMEM(shape, dt) for _ in range(DECODE_PARTS)]
    return (per_part((KV_WIDTH, tkp), BF16) + per_part((KV_WIDTH, tkp), BF16) + per_part((rows, tkp), F32)
            + per_part((rows, tkp), BF16) + per_part((rows, LANES), F32)
            + [pltpu.VMEM((rows, LANES), F32), pltpu.VMEM((rows, LANES), F32), pltpu.VMEM((rows, KV_WIDTH), F32)])


def _decode_chunk(k_refs, v_refs, qbd_ref, scratch, tile_bias, rows_all, rt):
    np_ = DECODE_PARTS
    kt_refs, vt_refs, s_refs, p_refs, a_refs = (scratch[np_ * n:np_ * (n + 1)] for n in range(5))
    m_ref, l_ref, acc_ref = scratch[5 * np_:]
    ppp = len(k_refs) // np_
    nl = ppp * PAGE // LANES
    for h in range(np_):
        for pg in range(ppp):
            kt_refs[h][:, PAGE * pg:PAGE * (pg + 1)] = k_refs[h * ppp + pg][0, 0].astype(BF16)
            vt_refs[h][:, PAGE * pg:PAGE * (pg + 1)] = v_refs[h * ppp + pg][0, 0].astype(BF16)
        s_refs[h][...] = _dot(qbd_ref[...], kt_refs[h][...])
    for h in range(np_):
        for t in range(rows_all // rt):
            rows = slice(t * rt, (t + 1) * rt)
            tiles = [s_refs[h][rows, LANES * cl:LANES * (cl + 1)] + tile_bias(t, h * nl + cl) for cl in range(nl)]
            _softmax_tile(tiles, rows, m_ref, l_ref, p_refs[h], acc_ref, a_refs[h])
        pv = _nt(p_refs[h][...], vt_refs[h][...])
        alpha = a_refs[h][...]
        for cl in range(KV_WIDTH // LANES):
            lanes = slice(LANES * cl, LANES * (cl + 1))
            acc_ref[:, lanes] = acc_ref[:, lanes] * alpha + pv[:, lanes]


def _fox_decode_kernel(pt_ref, q_ref, kn_ref, vn_ref, lfn_ref, *rest, npg, t_new, rt):
    k_refs = rest[:npg]
    v_refs = rest[npg:2 * npg]
    lf_refs = rest[2 * npg:3 * npg]
    o_ref, qbd_ref, sfx_ref, carry_ref, gq_ref = rest[3 * npg:3 * npg + 5]
    scratch = rest[3 * npg + 5:]
    m_ref, l_ref, acc_ref = scratch[-3:]
    c = pl.program_id(1)
    nhq = FOX_HEADS
    rows_all = nhq * t_new
    tk = npg * PAGE
    hmask = _head_mask(t_new, nhq, lambda grp: grp // 2)

    @pl.when(c == 0)
    def _():
        q = q_ref[...] * FOX_SCALE
        qrep = jnp.concatenate([q[:, KV_WIDTH * (hq % 2):KV_WIDTH * (hq % 2 + 1)] for hq in range(nhq)], axis=0)
        qbd16 = jnp.where(hmask, qrep, 0.0).astype(BF16)
        qbd_ref[...] = qbd16
        eye = (_iota((nhq, nhq), 0) == _iota((nhq, nhq), 1)).astype(F32)
        lfn = lfn_ref[...]
        tri = (_iota((t_new, t_new), 1) <= _iota((t_new, t_new), 0)).astype(F32)
        gct = _nt(eye, _dot(tri, lfn, precision=HI), precision=HI)
        g = jnp.concatenate([jnp.broadcast_to(gct[hq:hq + 1, :], (t_new, t_new)) for hq in range(nhq)], axis=0)
        tq = _iota((rows_all, t_new), 0) % t_new
        tl = _iota((rows_all, t_new), 1)
        gq = jnp.sum(jnp.where(tl == tq, g, 0.0), axis=-1, keepdims=True)
        gq_ref[...] = jnp.broadcast_to(gq, gq_ref.shape)
        carry_ref[...] = jnp.zeros(carry_ref.shape, F32)
        _new_token_init(qbd16, kn_ref[...], vn_ref[...], gq - g, m_ref, l_ref, acc_ref, t_new)

    lft = jnp.concatenate([r[0, 0] for r in lf_refs], axis=1)
    lane = _iota((nhq, tk), 1)
    incl = lft
    sh = 1
    while sh < tk:
        incl = incl + jnp.where(lane < tk - sh, pltpu.roll(incl, tk - sh, axis=1), 0.0)
        sh *= 2
    sfx_ref[...] = incl - lft + carry_ref[...]
    carry_ref[...] = carry_ref[...] + incl[:, 0:1]
    hq_per_tile = rt // t_new

    def tile_bias(t, cl):
        lanes = slice(LANES * cl, LANES * (cl + 1))
        sfx = jnp.concatenate([jnp.broadcast_to(sfx_ref[hq:hq + 1, lanes], (t_new, LANES))
                               for hq in range(t * hq_per_tile, (t + 1) * hq_per_tile)], axis=0)
        return sfx + gq_ref[t * rt:(t + 1) * rt, :]

    _decode_chunk(k_refs, v_refs, qbd_ref, scratch, tile_bias, rows_all, rt)

    @pl.when(c == pl.num_programs(1) - 1)
    def _():
        o = jnp.where(hmask, acc_ref[...] / l_ref[:, 0:1], 0.0)
        for g in range(2):
            tot = jnp.zeros((t_new, KV_WIDTH), F32)
            for kvh in range(FOX_KV_HEADS):
                r0 = (kvh * 2 + g) * t_new
                tot = tot + o[r0:r0 + t_new, :]
            o_ref[g] = tot


def _fox_decode(q, k_new, v_new, lf_new, pool_k_t, pool_v_t, pool_lf_t, layer, page_table, t_new):
    nreq, n_pages = page_table.shape
    npg = min(PAGES_PER_STEP, n_pages)
    nchunk = n_pages // npg
    rows = FOX_HEADS * t_new
    tk = npg * PAGE
    new = lambda w: pl.BlockSpec((t_new, w), lambda b, c, pt: (b, 0))
    grid_spec = pltpu.PrefetchScalarGridSpec(
        num_scalar_prefetch=1,
        grid=(nreq, nchunk),
        in_specs=[new(1024), new(KV_WIDTH), new(KV_WIDTH), new(FOX_HEADS)]
        + _page_specs((KV_WIDTH, PAGE), npg, layer, nchunk) + _page_specs((KV_WIDTH, PAGE), npg, layer, nchunk)
        + _page_specs((FOX_HEADS, PAGE), npg, layer, nchunk),
        out_specs=pl.BlockSpec((2, t_new, KV_WIDTH), lambda b, c, pt: (0, b, 0)),
        scratch_shapes=[pltpu.VMEM((rows, KV_WIDTH), BF16), pltpu.VMEM((FOX_HEADS, tk), F32),
                        pltpu.VMEM((FOX_HEADS, 1), F32), pltpu.VMEM((rows, LANES), F32)]
        + _decode_scratch(rows, tk),
    )
    return pl.pallas_call(
        functools.partial(_fox_decode_kernel, npg=npg, t_new=t_new, rt=16),
        grid_spec=grid_spec,
        out_shape=jax.ShapeDtypeStruct((2, nreq * t_new, KV_WIDTH), F32),
        compiler_params=_params("parallel", "arbitrary"),
        name="fox_decode",
    )(page_table, q, k_new, v_new, lf_new, *([pool_k_t] * npg), *([pool_v_t] * npg), *([pool_lf_t] * npg))


def _moba_decode_kernel(pt_ref, q_ref, kn_ref, vn_ref, mean_ref, slope_ref, *rest, npg, t_new, past, rt):
    k_refs = rest[:npg]
    v_refs = rest[npg:2 * npg]
    o_ref, qbd_ref, sel_ref, ab_ref, sl_ref = rest[2 * npg:2 * npg + 5]
    scratch = rest[2 * npg + 5:]
    m_ref, l_ref, acc_ref = scratch[-3:]
    c = pl.program_id(1)
    nh = MOBA_HEADS
    rows_all = nh * t_new
    tk = npg * PAGE
    bpc = tk // MOBA_BLOCK
    lpb = MOBA_BLOCK // LANES
    nblk = mean_ref.shape[2]
    hmask = _head_mask(t_new, nh, lambda grp: grp)

    @pl.when(c == 0)
    def _():
        q = q_ref[...]
        qbd = jnp.where(hmask, jnp.concatenate([q] * nh, axis=0), 0.0)
        qbd16 = (qbd * MOBA_SCALE).astype(BF16)
        qbd_ref[...] = qbd16
        sel_ref[...] = _top3_mask(_dot(qbd, mean_ref[0], precision=HI), nblk)
        slope = slope_ref[...]
        t_of_row = _iota((rows_all, 1), 0) % t_new
        ab_ref[...] = jnp.broadcast_to(-slope * (past + t_of_row).astype(F32), ab_ref.shape)
        sl_ref[...] = jnp.broadcast_to(slope, sl_ref.shape)
        bias_new = -slope * (t_of_row - _iota((1, t_new), 1)).astype(F32)
        _new_token_init(qbd16, kn_ref[...], vn_ref[...], bias_new, m_ref, l_ref, acc_ref, t_new)

    keep_bias = {}

    def tile_bias(t, cl):
        rows = slice(t * rt, (t + 1) * rt)
        j = cl // lpb
        if (t, j) not in keep_bias:
            chosen = jnp.sum(jnp.where(_iota((rt, nblk), 1) == c * bpc + j, sel_ref[rows, :], 0.0),
                             axis=-1, keepdims=True)
            keep_bias[(t, j)] = jnp.where(chosen > 0.0, 0.0, NEG)
        kposf = (c * tk + LANES * cl + _iota((1, LANES), 1)).astype(F32)
        return (ab_ref[rows, :] + sl_ref[rows, :] * kposf) + keep_bias[(t, j)]

    _decode_chunk(k_refs, v_refs, qbd_ref, scratch, tile_bias, rows_all, rt)

    @pl.when(c == pl.num_programs(1) - 1)
    def _():
        o = jnp.where(hmask, acc_ref[...] / l_ref[:, 0:1], 0.0)
        tot = jnp.zeros((t_new, KV_WIDTH), F32)
        for h in range(nh):
            tot = tot + o[h * t_new:(h + 1) * t_new, :]
        o_ref[...] = tot


def _moba_decode(q, k_new, v_new, means_t, slopes, pool_k_t, pool_v_t, layer, page_table, t_new):
    nreq, n_pages = page_table.shape
    npg = min(PAGES_PER_STEP, n_pages)
    nchunk = n_pages // npg
    rows = MOBA_HEADS * t_new
    tk = npg * PAGE
    nblk = means_t.shape[2]
    new = lambda w: pl.BlockSpec((t_new, w), lambda b, c, pt: (b, 0))
    grid_spec = pltpu.PrefetchScalarGridSpec(
        num_scalar_prefetch=1,
        grid=(nreq, nchunk),
        in_specs=[new(KV_WIDTH), new(KV_WIDTH), new(KV_WIDTH),
                  pl.BlockSpec((1, KV_WIDTH, nblk), lambda b, c, pt: (b, 0, 0)),
                  pl.BlockSpec((rows, 1), lambda b, c, pt: (0, 0))]
        + _page_specs((KV_WIDTH, PAGE), npg, layer) + _page_specs((KV_WIDTH, PAGE), npg, layer),
        out_specs=pl.BlockSpec((t_new, KV_WIDTH), lambda b, c, pt: (b, 0)),
        scratch_shapes=[pltpu.VMEM((rows, KV_WIDTH), BF16), pltpu.VMEM((rows, nblk), F32),
                        pltpu.VMEM((rows, LANES), F32), pltpu.VMEM((rows, LANES), F32)]
        + _decode_scratch(rows, tk),
    )
    return pl.pallas_call(
        functools.partial(_moba_decode_kernel, npg=npg, t_new=t_new, past=n_pages * PAGE, rt=16),
        grid_spec=grid_spec,
        out_shape=jax.ShapeDtypeStruct((nreq * t_new, KV_WIDTH), F32),
        compiler_params=_params("parallel", "arbitrary"),
        name="moba_decode",
    )(page_table, q, k_new, v_new, means_t, slopes, *([pool_k_t] * npg), *([pool_v_t] * npg))


def _rot_cols(w):
    half = MLA_ROPE // 2
    return jnp.concatenate([-w[..., half:], w[..., :half]], axis=-1)


def _pad_cols(w, width):
    return jnp.pad(w, [(0, 0)] * (w.ndim - 1) + [(0, width - w.shape[-1])])


def _prep_even(w_in, w_uq, w_uk, w_uv):
    o = MLA_Q_RANK + MLA_KV_RANK
    pe = w_in[:, o:o + MLA_ROPE]
    w_lat = w_in[:, :o].astype(BF16)
    w_pe = jnp.concatenate([_pad_cols(pe, 128), _pad_cols(_rot_cols(pe), 128)], axis=1).astype(BF16)
    w_moba = w_in[:, o + MLA_ROPE:].astype(BF16)
    nope = w_uq[:, :, :MLA_NOPE].reshape(MLA_Q_RANK, MLA_HEADS * MLA_NOPE)
    qpe = w_uq[:, :, MLA_NOPE:]
    w_uq_x = jnp.concatenate([nope, _pad_cols(qpe, 128).reshape(MLA_Q_RANK, -1),
                              _pad_cols(_rot_cols(qpe), 128).reshape(MLA_Q_RANK, -1)], axis=1).astype(BF16)
    uk = jnp.transpose(w_uk, (1, 2, 0))
    uv = jnp.transpose(w_uv, (1, 0, 2))
    z_k = jnp.zeros((MLA_NOPE, MLA_KV_RANK), F32)
    z_v = jnp.zeros((MLA_KV_RANK, MLA_V), F32)
    uk_bd = jnp.stack([jnp.block([[uk[2 * p], z_k], [z_k, uk[2 * p + 1]]]) for p in range(MLA_HEADS // 2)])
    uv_bd = jnp.stack([jnp.block([[uv[2 * p], z_v], [z_v, uv[2 * p + 1]]]) for p in range(MLA_HEADS // 2)])
    return (w_lat, w_pe, w_moba), w_uq_x, uk_bd.astype(BF16), uv_bd.astype(BF16)


def _prep_odd(w_in, w_out):
    nq = FOX_HEADS * HEAD_DIM
    nk = FOX_KV_HEADS * HEAD_DIM
    wq = w_in[:, :nq].reshape(D_MODEL, FOX_KV_HEADS, 2, HEAD_DIM).transpose(0, 2, 1, 3).reshape(D_MODEL, nq)
    w_kv = w_in[:, nq:nq + 2 * nk]
    w_f = _pad_cols(w_in[:, nq + 2 * nk:], 128)
    w_out_x = w_out.reshape(FOX_KV_HEADS, 2, HEAD_DIM, D_MODEL).transpose(1, 0, 2, 3).reshape(nq, D_MODEL)
    return (wq.astype(BF16), w_kv.astype(BF16), w_f.astype(BF16)), w_out_x.astype(BF16)


def _rope_tables(pos):
    half = MLA_ROPE // 2
    inv = ROPE_THETA ** (-jnp.arange(half, dtype=F32) / half)
    ang = pos.astype(F32)[:, None] * inv[None, :]
    cos, sin = jnp.cos(ang), jnp.sin(ang)
    return (_pad_cols(jnp.concatenate([cos, cos], axis=-1), 128),
            _pad_cols(jnp.concatenate([sin, sin], axis=-1), 128))


def _row_tile(m):
    for t in (512, 256, 128, 64, 32, 16, 8):
        if m % t == 0:
            return t
    raise ValueError(f"row count {m} is not a multiple of 8")


def _kv_pool_t(pool):
    l, n, pg, h, d = pool.shape
    return jnp.transpose(pool, (0, 1, 3, 4, 2)).reshape(l, n, h * d, pg)


def kernel(x_prompt, x_sample, cache_mla_ckv, cache_mla_kpe, cache_moba_k, cache_moba_v, cache_fox_k, cache_fox_v, cache_fox_logf, page_table, attn_norm, ffn_norm, final_norm, w_in_even, mla_q_norm, mla_kv_norm, mla_w_uq, mla_w_uk, mla_w_uv, w_out_even, w_in_odd, fox_b_forget, w_out_odd, ffn_w_gate, ffn_w_up, ffn_w_down):
    nb, seq, d = x_prompt.shape
    nreq, t_new, _ = x_sample.shape
    n_pages = page_table.shape[1]
    past = n_pages * PAGE
    depth = attn_norm.shape[0]
    assert d == D_MODEL and seq % MOBA_BLOCK == 0 and past % MOBA_BLOCK == 0 and t_new == 8
    assert n_pages % min(PAGES_PER_STEP, n_pages) == 0 and n_pages % min(MLA_PAGES_PER_STEP, n_pages) == 0
    mp, ms = nb * seq, nreq * t_new
    tmp, tms = _row_tile(mp), _row_tile(ms)

    xp = x_prompt.reshape(mp, d)
    xs = x_sample.reshape(ms, d)
    cs_p, sn_p = (jnp.tile(t, (nb, 1)) for t in _rope_tables(jnp.arange(seq, dtype=jnp.int32)))
    cs_s, sn_s = (jnp.tile(t, (nreq, 1)) for t in _rope_tables(past + jnp.arange(t_new, dtype=jnp.int32)))
    slope_h = jnp.exp2(-8.0 * jnp.arange(1, MOBA_HEADS + 1, dtype=F32) / MOBA_HEADS)
    slopes_p = jnp.repeat(slope_h, MOBA_BLOCK).reshape(MOBA_HEADS // 2, 2 * MOBA_BLOCK, 1)
    slopes_s = jnp.repeat(slope_h, t_new).reshape(MOBA_HEADS * t_new, 1)
    pool_moba_k = _kv_pool_t(cache_moba_k)
    pool_moba_v = _kv_pool_t(cache_moba_v)
    pool_fox_k = _kv_pool_t(cache_fox_k)
    pool_fox_v = _kv_pool_t(cache_fox_v)
    pool_fox_lf = jnp.transpose(cache_fox_logf, (0, 1, 3, 2))
    pool_mla_kpe = jnp.transpose(cache_mla_kpe, (0, 1, 3, 2))
    row1 = lambda v: v.reshape(1, -1)

    outs = {k: [] for k in ("ckv_p", "ckv_s", "kpe_p", "kpe_s", "mk_p", "mk_s", "mv_p", "mv_s",
                            "fk_p", "fk_s", "fv_p", "fv_s", "fl_p", "fl_s")}
    for l in range(depth):
        final = l == depth - 1
        ffn = (row1(ffn_norm[l]), ffn_w_gate[l].astype(BF16), ffn_w_up[l].astype(BF16),
               ffn_w_down[l].astype(BF16), row1(final_norm))
        if l % 2 == 0:
            e = l // 2
            w_in_x, w_uq_x, uk_bd, uv_bd = _prep_even(w_in_even[e], mla_w_uq[e], mla_w_uk[e], mla_w_uv[e])
            w_out = w_out_even[e].astype(BF16)
            common = (row1(attn_norm[l]), *w_in_x, row1(mla_q_norm[e]), row1(mla_kv_norm[e]), w_uq_x, uk_bd)
            qcat, kcat, ckv, kpe, mq, mk, mv = _proj_even(xp, *common, cs_p, sn_p, tmp, BF16)
            mix_a = _mla_prompt(qcat, kcat, uv_bd, nb, seq)
            mix_b = _moba_prompt(mq, mk, mv, slopes_p, nb, seq)
            xp = _out_ffn(xp, mix_a, mix_b, w_out, *ffn, tmp, final)
            outs["ckv_p"].append(ckv.reshape(nb, seq, MLA_KV_RANK))
            outs["kpe_p"].append(kpe.reshape(nb, seq, MLA_ROPE))
            outs["mk_p"].append(mk.reshape(nb, seq, MOBA_HEADS, HEAD_DIM))
            outs["mv_p"].append(mv.reshape(nb, seq, MOBA_HEADS, HEAD_DIM))

            qcat, kcat, ckv, kpe, mq, mk, mv = _proj_even(xs, *common, cs_s, sn_s, tms, F32)
            mix_a = _mla_decode(qcat, kcat, uv_bd, cache_mla_ckv, pool_mla_kpe, e, page_table, t_new)
            means_t = _moba_means(pool_moba_k, e, page_table)
            mix_b = _moba_decode(mq, mk, mv, means_t, slopes_s, pool_moba_k, pool_moba_v, e, page_table, t_new)
            xs = _out_ffn(xs, mix_a, mix_b, w_out, *ffn, tms, final)
            outs["ckv_s"].append(ckv.reshape(nreq, t_new, MLA_KV_RANK))
            outs["kpe_s"].append(kpe.reshape(nreq, t_new, MLA_ROPE))
            outs["mk_s"].append(mk.reshape(nreq, t_new, MOBA_HEADS, HEAD_DIM))
            outs["mv_s"].append(mv.reshape(nreq, t_new, MOBA_HEADS, HEAD_DIM))
        else:
            o = l // 2
            w_in_x, w_out = _prep_odd(w_in_odd[o], w_out_odd[o])
            common = (row1(attn_norm[l]), *w_in_x, row1(fox_b_forget[o]))
            q, k, v, lf = _proj_odd(xp, *common, tmp, BF16)
            f, ft = _fox_cumsum(lf, lf.reshape(nb, seq, FOX_HEADS).transpose(0, 2, 1), nb, seq)
            fq = f.reshape(mp, 4, 4).transpose(1, 0, 2)
            mix = _fox_prompt(q, k, v, fq, ft.reshape(nb, 4, 4, seq), nb, seq)
            xp = _out_ffn(xp, mix[0], mix[1], w_out, *ffn, tmp, final)
            outs["fk_p"].append(k.reshape(nb, seq, FOX_KV_HEADS, HEAD_DIM))
            outs["fv_p"].append(v.reshape(nb, seq, FOX_KV_HEADS, HEAD_DIM))
            outs["fl_p"].append(lf.reshape(nb, seq, FOX_HEADS))

            q, k, v, lf = _proj_odd(xs, *common, tms, F32)
            mix = _fox_decode(q, k, v, lf, pool_fox_k, pool_fox_v, pool_fox_lf, o, page_table, t_new)
            xs = _out_ffn(xs, mix[0], mix[1], w_out, *ffn, tms, final)
            outs["fk_s"].append(k.reshape(nreq, t_new, FOX_KV_HEADS, HEAD_DIM))
            outs["fv_s"].append(v.reshape(nreq, t_new, FOX_KV_HEADS, HEAD_DIM))
            outs["fl_s"].append(lf.reshape(nreq, t_new, FOX_HEADS))

    st = lambda name: jnp.stack(outs[name])
    return (xp.reshape(nb, seq, d), xs.reshape(nreq, t_new, d),
            st("ckv_p"), st("ckv_s"), st("kpe_p"), st("kpe_s"), st("mk_p"), st("mk_s"), st("mv_p"), st("mv_s"),
            st("fk_p"), st("fk_s"), st("fv_p"), st("fv_s"), st("fl_p"), st("fl_s"))
```

```python
import functools

import jax
import jax.numpy as jnp
from jax import lax
from jax.experimental import pallas as pl
from jax.experimental.pallas import tpu as pltpu

F32 = jnp.float32
BF16 = jnp.bfloat16

D_MODEL = 1024
HEAD_DIM = 64
PAGE = 128
RMS_EPS = 1e-6
MLA_HEADS = 8
MLA_Q_RANK = 256
MLA_KV_RANK = 128
MLA_NOPE = 64
MLA_ROPE = 32
MLA_V = 64
ROPE_THETA = 10000.0
MLA_SCALE = (MLA_NOPE + MLA_ROPE) ** -0.5
MOBA_HEADS = 8
MOBA_BLOCK = 256
MOBA_TOPK = 3
MOBA_SCALE = HEAD_DIM ** -0.5
FOX_HEADS = 16
FOX_KV_HEADS = 8
FOX_SCALE = HEAD_DIM ** -0.5
FFN_HIDDEN = 2816
FFN_CHUNK = 1408
LANES = 128
KV_WIDTH = 512
NEG = -1e30
VMEM_LIMIT = 56 * 1024 * 1024
PAGES_PER_STEP = 32
MLA_PAGES_PER_STEP = 64

HI = lax.Precision.HIGHEST


def _nt(a, b, precision=None):
    return lax.dot_general(a, b, (((1,), (1,)), ((), ())), precision=precision,
                           preferred_element_type=F32)


def _dot(a, b, precision=None):
    return jnp.dot(a, b, precision=precision, preferred_element_type=F32)


def _rms(x, g):
    return x * lax.rsqrt(jnp.mean(x * x, axis=-1, keepdims=True) + RMS_EPS) * g


def _params(*sem):
    return pltpu.CompilerParams(dimension_semantics=sem, vmem_limit_bytes=VMEM_LIMIT)


def _const_spec(shape):
    nd = len(shape)
    return pl.BlockSpec(shape, lambda *a: (0,) * nd, pipeline_mode=pl.Buffered(1))


def _iota(shape, dim):
    return lax.broadcasted_iota(jnp.int32, shape, dim)


def _proj_even_kernel(x_ref, g_ref, wlat_ref, wpe_ref, wmoba_ref, qn_ref, kvn_ref, wuq_ref, wuk_ref, cs_ref, sn_ref,
                      qcat_ref, kcat_ref, ckv_ref, kpe_ref, mq_ref, mk_ref, mv_ref):
    h = _rms(x_ref[...], g_ref[...]).astype(BF16)
    cq = _rms(_dot(h, wlat_ref[:, 0:256]), qn_ref[...]).astype(BF16)
    ckv = _rms(_dot(h, wlat_ref[:, 256:384]), kvn_ref[...])
    cs = cs_ref[...]
    sn = sn_ref[...]
    kpe = _dot(h, wpe_ref[:, 0:128]) * cs + _dot(h, wpe_ref[:, 128:256]) * sn
    ckv_ref[...] = ckv
    kpe_ref[...] = kpe[:, :MLA_ROPE]
    kcat_ref[:, 0:128] = ckv.astype(kcat_ref.dtype)
    kcat_ref[:, 128:256] = kpe.astype(kcat_ref.dtype)
    mq_ref[...] = _dot(h, wmoba_ref[:, 0:512])
    mk_ref[...] = _dot(h, wmoba_ref[:, 512:1024])
    mv_ref[...] = _dot(h, wmoba_ref[:, 1024:1536])
    qn = _dot(cq, wuq_ref[:, 0:512]).astype(BF16)
    for p in range(MLA_HEADS // 2):
        ql = _dot(qn[:, 128 * p:128 * p + 128], wuk_ref[p])
        for hh in range(2):
            hd = 2 * p + hh
            qpe = (_dot(cq, wuq_ref[:, 512 + 128 * hd:640 + 128 * hd]) * cs
                   + _dot(cq, wuq_ref[:, 1536 + 128 * hd:1664 + 128 * hd]) * sn)
            qcat_ref[:, 256 * hd:256 * hd + 128] = (ql[:, 128 * hh:128 * hh + 128] * MLA_SCALE).astype(qcat_ref.dtype)
            qcat_ref[:, 256 * hd + 128:256 * hd + 256] = (qpe * MLA_SCALE).astype(qcat_ref.dtype)


def _kv_stack_call(kernel_fn, name, grid, in_specs, out_specs, out_shape, kv_outs, stack, args):
    n_in = len(args)
    aliases = {}
    if stack is not None:
        in_specs = list(in_specs) + [pl.BlockSpec(memory_space=pl.ANY)] * len(stack)
        aliases = {n_in + n: o for n, o in enumerate(kv_outs)}
        args = tuple(args) + tuple(stack)

    def body(*refs):
        kernel_fn(*refs[:n_in], *refs[len(args):])

    return pl.pallas_call(body, grid=grid, in_specs=in_specs, out_specs=out_specs, out_shape=out_shape,
                          input_output_aliases=aliases, compiler_params=_params("parallel"), name=name)(*args)


def _proj_even(x, g, w_lat, w_pe, w_moba, qn, kvn, w_uq, w_uk, cs, sn, tm, act_dtype, layer, n_layers, stack):
    m = x.shape[0]
    row = lambda w: pl.BlockSpec((tm, w), lambda i: (i, 0))
    slab = pl.BlockSpec((None, tm, 512), lambda i: (layer, i, 0))
    out_shape = (
        jax.ShapeDtypeStruct((m, 2048), act_dtype),
        jax.ShapeDtypeStruct((m, 256), act_dtype),
        jax.ShapeDtypeStruct((m, 128), F32),
        jax.ShapeDtypeStruct((m, 32), F32),
        jax.ShapeDtypeStruct((m, 512), F32),
        jax.ShapeDtypeStruct((n_layers, m, 512), F32),
        jax.ShapeDtypeStruct((n_layers, m, 512), F32),
    )
    in_specs = [row(D_MODEL), _const_spec((1, D_MODEL)), _const_spec(w_lat.shape), _const_spec(w_pe.shape),
                _const_spec(w_moba.shape), _const_spec((1, 256)), _const_spec((1, 128)),
                _const_spec(w_uq.shape), _const_spec(w_uk.shape), row(128), row(128)]
    return _kv_stack_call(_proj_even_kernel, "proj_even", (m // tm,), in_specs,
                          (row(2048), row(256), row(128), row(32), row(512), slab, slab), out_shape, (5, 6), stack,
                          (x, g, w_lat, w_pe, w_moba, qn, kvn, w_uq, w_uk, cs, sn))


def _proj_odd_kernel(x_ref, g_ref, wq_ref, wkv_ref, wf_ref, bf_ref, q_ref, k_ref, v_ref, lf_ref):
    h = _rms(x_ref[...], g_ref[...]).astype(BF16)
    q_ref[...] = _dot(h, wq_ref[...]).astype(q_ref.dtype)
    k_ref[...] = _dot(h, wkv_ref[:, 0:512])
    v_ref[...] = _dot(h, wkv_ref[:, 512:1024])
    zf = _dot(h, wf_ref[...])[:, :FOX_HEADS] + bf_ref[...]
    lf_ref[...] = -(jnp.maximum(-zf, 0.0) + jnp.log1p(jnp.exp(-jnp.abs(zf))))


def _proj_odd(x, g, w_q, w_kv, w_f, b_f, tm, act_dtype, layer, n_layers, stack):
    m = x.shape[0]
    row = lambda w: pl.BlockSpec((tm, w), lambda i: (i, 0))
    slab = pl.BlockSpec((None, tm, 512), lambda i: (layer, i, 0))
    out_shape = (
        jax.ShapeDtypeStruct((m, 1024), act_dtype),
        jax.ShapeDtypeStruct((n_layers, m, 512), F32),
        jax.ShapeDtypeStruct((n_layers, m, 512), F32),
        jax.ShapeDtypeStruct((m, FOX_HEADS), F32),
    )
    in_specs = [row(D_MODEL), _const_spec((1, D_MODEL)), _const_spec(w_q.shape), _const_spec(w_kv.shape),
                _const_spec(w_f.shape), _const_spec((1, FOX_HEADS))]
    return _kv_stack_call(_proj_odd_kernel, "proj_odd", (m // tm,), in_specs,
                          (row(1024), slab, slab, row(FOX_HEADS)), out_shape, (1, 2), stack,
                          (x, g, w_q, w_kv, w_f, b_f))


def _out_ffn_kernel(x_ref, a_ref, b_ref, wo_ref, g_ref, wg_ref, wu_ref, wd_ref, fn_ref, o_ref, *, final):
    x1 = (x_ref[...] + _dot(a_ref[...].astype(BF16), wo_ref[0:512, :])
          + _dot(b_ref[...].astype(BF16), wo_ref[512:1024, :]))
    h = _rms(x1, g_ref[...]).astype(BF16)
    acc = jnp.zeros_like(x1)
    for c in range(FFN_HIDDEN // FFN_CHUNK):
        lo = c * FFN_CHUNK
        gate = _dot(h, wg_ref[:, lo:lo + FFN_CHUNK])
        up = _dot(h, wu_ref[:, lo:lo + FFN_CHUNK])
        act = (gate / (1.0 + jnp.exp(-gate))) * up
        acc = acc + _dot(act.astype(BF16), wd_ref[lo:lo + FFN_CHUNK, :])
    out = x1 + acc
    if final:
        out = _rms(out, fn_ref[...])
    o_ref[...] = out


def _out_ffn(x, mix_a, mix_b, w_out, g, w_gate, w_up, w_down, final_norm, tm, final):
    m = x.shape[0]
    row = lambda w: pl.BlockSpec((tm, w), lambda i: (i, 0))
    return pl.pallas_call(
        functools.partial(_out_ffn_kernel, final=final),
        grid=(m // tm,),
        in_specs=[row(D_MODEL), row(512), row(512), _const_spec(w_out.shape), _const_spec((1, D_MODEL)),
                  _const_spec(w_gate.shape), _const_spec(w_up.shape), _const_spec(w_down.shape),
                  _const_spec((1, D_MODEL))],
        out_specs=row(D_MODEL),
        out_shape=jax.ShapeDtypeStruct((m, D_MODEL), F32),
        compiler_params=_params("parallel"),
        name="out_ffn",
    )(x, mix_a, mix_b, w_out, g, w_gate, w_up, w_down, final_norm)


def _init_stats(m_ref, l_ref, acc_ref):
    m_ref[...] = jnp.full(m_ref.shape, NEG, F32)
    l_ref[...] = jnp.zeros(l_ref.shape, F32)
    acc_ref[...] = jnp.zeros(acc_ref.shape, F32)


def _softmax_tile(tiles, rows, m_ref, l_ref, p_ref, acc_ref, alpha_ref=None):
    mx = tiles[0]
    for t in tiles[1:]:
        mx = jnp.maximum(mx, t)
    m_old = m_ref[rows, :]
    m_new = jnp.maximum(m_old, jnp.max(mx, axis=-1, keepdims=True))
    alpha = jnp.exp(m_old - m_new)
    ps = None
    for c, t in enumerate(tiles):
        p = jnp.exp(t - m_new)
        ps = p if ps is None else ps + p
        p_ref[rows, LANES * c:LANES * (c + 1)] = p.astype(p_ref.dtype)
    l_ref[rows, :] = alpha * l_ref[rows, :] + jnp.sum(ps, axis=-1, keepdims=True)
    m_ref[rows, :] = m_new
    if alpha_ref is not None:
        alpha_ref[rows, :] = alpha
        return
    for c in range(acc_ref.shape[1] // LANES):
        acc_ref[rows, LANES * c:LANES * (c + 1)] = acc_ref[rows, LANES * c:LANES * (c + 1)] * alpha


def _causal_blocks(n_plain, qk, softmax, pv):
    qk(0, 0)

    def pair(jj, carry):
        j = 2 * jj
        qk(j + 1, 1)
        softmax(j, 0, False)
        pv(j, 0)
        qk(j + 2, 0)
        softmax(j + 1, 1, False)
        pv(j + 1, 1)
        return carry

    lax.fori_loop(0, n_plain // 2, pair, 0)

    @pl.when(n_plain % 2 == 0)
    def _():
        softmax(n_plain, 0, True)
        pv(n_plain, 0)

    @pl.when(n_plain % 2 == 1)
    def _():
        qk(n_plain, 1)
        softmax(n_plain - 1, 0, False)
        pv(n_plain - 1, 0)
        softmax(n_plain, 1, True)
        pv(n_plain, 1)


def _flash_scratch(rows, tk, width):
    return [pltpu.VMEM((rows, tk), F32), pltpu.VMEM((rows, tk), F32),
            pltpu.VMEM((rows, tk), BF16), pltpu.VMEM((rows, tk), BF16),
            pltpu.VMEM((rows, LANES), F32), pltpu.VMEM((rows, LANES), F32),
            pltpu.VMEM((rows, LANES), F32), pltpu.VMEM((rows, LANES), F32), pltpu.VMEM((rows, width), F32)]


def _mla_prompt_kernel(q_ref, k_ref, wuv_ref, o_ref, qs_ref, s0_ref, s1_ref, p0_ref, p1_ref, a0_ref, a1_ref,
                       m_ref, l_ref, acc_ref, *, tq, tk, rt):
    i = pl.program_id(1)
    nh = MLA_HEADS
    nl = tk // LANES
    s_refs, p_refs, a_refs = (s0_ref, s1_ref), (p0_ref, p1_ref), (a0_ref, a1_ref)
    for h in range(nh):
        qs_ref[h * tq:(h + 1) * tq, :] = q_ref[:, 256 * h:256 * h + 256]
    _init_stats(m_ref, l_ref, acc_ref)
    j_last = (i * tq + tq - 1) // tk

    def qk(j, buf):
        kb = k_ref[pl.ds(pl.multiple_of(j * tk, tk), tk), :]
        s_refs[buf][...] = _nt(qs_ref[...], kb)

    def pv(j, buf):
        vb = k_ref[pl.ds(pl.multiple_of(j * tk, tk), tk), 0:MLA_KV_RANK]
        acc_ref[...] = acc_ref[...] * a_refs[buf][...] + _dot(p_refs[buf][...], vb)

    def softmax(j, buf, masked):
        for t in range(nh * tq // rt):
            rows = slice(t * rt, (t + 1) * rt)
            tiles = []
            for c in range(nl):
                s = s_refs[buf][rows, LANES * c:LANES * (c + 1)]
                if masked:
                    qpos = i * tq + (t * rt) % tq + _iota((rt, 1), 0)
                    kpos = j * tk + LANES * c + _iota((1, LANES), 1)
                    s = jnp.where(kpos <= qpos, s, NEG)
                tiles.append(s)
            _softmax_tile(tiles, rows, m_ref, l_ref, p_refs[buf], acc_ref, a_refs[buf])

    _causal_blocks(j_last, qk, softmax, pv)
    o = acc_ref[...] / l_ref[...]
    for p in range(nh // 2):
        pair = jnp.concatenate([o[(2 * p) * tq:(2 * p + 1) * tq], o[(2 * p + 1) * tq:(2 * p + 2) * tq]], axis=1)
        o_ref[:, 128 * p:128 * p + 128] = _dot(pair.astype(BF16), wuv_ref[p]).astype(o_ref.dtype)


def _mla_prompt(qcat, kcat, w_uv_bd, nb, seq):
    tq, tk, rt = 128, 256, 32
    nq = seq // tq
    rows = MLA_HEADS * tq
    return pl.pallas_call(
        functools.partial(_mla_prompt_kernel, tq=tq, tk=tk, rt=rt),
        grid=(nb, nq),
        in_specs=[pl.BlockSpec((tq, 2048), lambda b, i: (b * nq + i, 0)),
                  pl.BlockSpec((seq, 256), lambda b, i: (b, 0)),
                  _const_spec(w_uv_bd.shape)],
        out_specs=pl.BlockSpec((tq, 512), lambda b, i: (b * nq + i, 0)),
        out_shape=jax.ShapeDtypeStruct((nb * seq, 512), BF16),
        scratch_shapes=[pltpu.VMEM((rows, 256), BF16)] + _flash_scratch(rows, tk, MLA_KV_RANK),
        compiler_params=_params("parallel", "arbitrary"),
        name="mla_prompt",
    )(qcat, kcat, w_uv_bd)


def _top3_mask(scores, n_valid):
    rows, nblk = scores.shape
    lane = _iota((rows, nblk), 1).astype(F32)
    avail = jnp.where(lane < n_valid, 1.0, 0.0)
    sel = jnp.zeros((rows, nblk), F32)
    for _ in range(MOBA_TOPK):
        work = jnp.where(avail > 0.0, scores, -jnp.inf)
        mx = jnp.max(work, axis=-1, keepdims=True)
        cand = jnp.where(work == mx, jnp.where(avail > 0.0, lane, float(nblk)), float(nblk))
        idx = jnp.min(cand, axis=-1, keepdims=True)
        pick = lane == idx
        sel = jnp.where(pick, 1.0, sel)
        avail = jnp.where(pick, 0.0, avail)
    return sel


def _moba_prompt_kernel(q_ref, k_ref, v_ref, slope_ref, o_ref, mean_ref, qbd_ref, sel_ref, ab_ref,
                        s0_ref, s1_ref, p0_ref, p1_ref, a0_ref, a1_ref, m_ref, l_ref, acc_ref, *, tq, nblk, rt):
    i = pl.program_id(2)
    blk = MOBA_BLOCK
    nl = blk // LANES
    rows_all = 2 * tq
    s_refs, p_refs, a_refs = (s0_ref, s1_ref), (p0_ref, p1_ref), (a0_ref, a1_ref)

    @pl.when(i == 0)
    def _():
        for j in range(nblk):
            mean_ref[j:j + 1, :] = jnp.sum(k_ref[j * blk:(j + 1) * blk, :], axis=0, keepdims=True) * (1.0 / blk)

    q = q_ref[...]
    lane = _iota((tq, LANES), 1)
    qbd = jnp.concatenate([jnp.where(lane < HEAD_DIM, q, 0.0), jnp.where(lane >= HEAD_DIM, q, 0.0)], axis=0)
    qbd_ref[...] = (qbd * MOBA_SCALE).astype(BF16)
    sel_ref[...] = _top3_mask(_nt(qbd, mean_ref[...], precision=HI), i)
    qposf = (i * tq + _iota((rows_all, 1), 0) % tq).astype(F32)
    ab_ref[...] = jnp.broadcast_to(-slope_ref[0] * qposf, (rows_all, LANES))
    _init_stats(m_ref, l_ref, acc_ref)

    def qk(j, buf):
        kb = k_ref[pl.ds(pl.multiple_of(j * blk, blk), blk), :].astype(BF16)
        s_refs[buf][...] = _nt(qbd_ref[...], kb)

    def pv(j, buf):
        vb = v_ref[pl.ds(pl.multiple_of(j * blk, blk), blk), :].astype(BF16)
        acc_ref[...] = acc_ref[...] * a_refs[buf][...] + _dot(p_refs[buf][...], vb)

    def softmax(j, buf, own):
        for g in range(2):
            sg = slope_ref[0, g * tq:g * tq + 1, :]
            cols = [sg * (j * blk + LANES * c + _iota((1, LANES), 1)).astype(F32) for c in range(nl)]
            for t in range(tq // rt):
                r0 = g * tq + t * rt
                rows = slice(r0, r0 + rt)
                rb = ab_ref[rows, :]
                if not own:
                    chosen = jnp.sum(jnp.where(_iota((rt, nblk), 1) == j, sel_ref[rows, :], 0.0),
                                     axis=-1, keepdims=True)
                    rb = rb + jnp.where(chosen > 0.0, 0.0, NEG)
                tiles = []
                for c in range(nl):
                    s = s_refs[buf][rows, LANES * c:LANES * (c + 1)] + (rb + cols[c])
                    if own:
                        s = jnp.where(LANES * c + _iota((1, LANES), 1) <= t * rt + _iota((rt, 1), 0), s, NEG)
                    tiles.append(s)
                _softmax_tile(tiles, rows, m_ref, l_ref, p_refs[buf], acc_ref, a_refs[buf])

    qk(i, 0)
    qk(0, 1)
    softmax(i, 0, True)
    pv(i, 0)

    def pair(jj, carry):
        j = 2 * jj
        qk(j + 1, 0)
        softmax(j, 1, False)
        pv(j, 1)
        qk(j + 2, 1)
        softmax(j + 1, 0, False)
        pv(j + 1, 0)
        return carry

    lax.fori_loop(0, i // 2, pair, 0)

    @pl.when(i % 2 == 1)
    def _():
        softmax(i - 1, 1, False)
        pv(i - 1, 1)

    o = acc_ref[...] / l_ref[...]
    o_ref[...] = jnp.where(lane < HEAD_DIM, o[0:tq], o[tq:2 * tq]).astype(o_ref.dtype)


def _moba_prompt(mq, mk, mv, slopes, nb, seq, layer):
    tq, rt = MOBA_BLOCK, 32
    nq = seq // tq
    nblk = seq // MOBA_BLOCK
    npair = MOBA_HEADS // 2
    rows = 2 * tq
    return pl.pallas_call(
        functools.partial(_moba_prompt_kernel, tq=tq, nblk=nblk, rt=rt),
        grid=(nb, npair, nq),
        in_specs=[pl.BlockSpec((tq, LANES), lambda b, p, i: (b * nq + i, p)),
                  pl.BlockSpec((None, seq, LANES), lambda b, p, i: (layer, b, p)),
                  pl.BlockSpec((None, seq, LANES), lambda b, p, i: (layer, b, p)),
                  pl.BlockSpec((1, rows, 1), lambda b, p, i: (p, 0, 0))],
        out_specs=pl.BlockSpec((tq, LANES), lambda b, p, i: (b * nq + i, p)),
        out_shape=jax.ShapeDtypeStruct((nb * seq, 512), BF16),
        scratch_shapes=[pltpu.VMEM((nblk, LANES), F32), pltpu.VMEM((rows, LANES), BF16),
                        pltpu.VMEM((rows, nblk), F32), pltpu.VMEM((rows, LANES), F32)]
        + _flash_scratch(rows, MOBA_BLOCK, LANES),
        compiler_params=_params("parallel", "parallel", "arbitrary"),
        name="moba_prompt",
    )(mq, mk, mv, slopes)


def _fox_cumsum_kernel(lf_ref, lft_ref, f_ref, ft_ref, *, seq, ch):
    r = _iota((ch, ch), 0)
    c = _iota((ch, ch), 1)
    lower = (c <= r).astype(F32)
    upper = (r <= c).astype(F32)
    carry = jnp.zeros((1, FOX_HEADS), F32)
    carry_t = jnp.zeros((FOX_HEADS, 1), F32)
    for j in range(seq // ch):
        f = _dot(lower, lf_ref[j * ch:(j + 1) * ch, :], precision=HI) + carry
        f_ref[j * ch:(j + 1) * ch, :] = f
        carry = f[ch - 1:ch, :]
        ft = _dot(lft_ref[0, :, j * ch:(j + 1) * ch], upper, precision=HI) + carry_t
        ft_ref[0, :, j * ch:(j + 1) * ch] = ft
        carry_t = ft[:, ch - 1:ch]


def _fox_cumsum(logf, logf_t, nb, seq):
    return pl.pallas_call(
        functools.partial(_fox_cumsum_kernel, seq=seq, ch=256),
        grid=(nb,),
        in_specs=[pl.BlockSpec((seq, FOX_HEADS), lambda b: (b, 0)),
                  pl.BlockSpec((1, FOX_HEADS, seq), lambda b: (b, 0, 0))],
        out_specs=(pl.BlockSpec((seq, FOX_HEADS), lambda b: (b, 0)),
                   pl.BlockSpec((1, FOX_HEADS, seq), lambda b: (b, 0, 0))),
        out_shape=(jax.ShapeDtypeStruct((nb * seq, FOX_HEADS), F32),
                   jax.ShapeDtypeStruct((nb, FOX_HEADS, seq), F32)),
        compiler_params=_params("parallel"),
        name="fox_cumsum",
    )(logf, logf_t)


def _fox_prompt_kernel(q0_ref, q1_ref, k_ref, v_ref, fq_ref, fk_ref, o_ref, qbd_ref, fqr_ref,
                       s0_ref, s1_ref, p0_ref, p1_ref, a0_ref, a1_ref, m_ref, l_ref, acc_ref, *, tq, rt):
    i = pl.program_id(2)
    tk = tq
    nl = tk // LANES
    s_refs, p_refs, a_refs = (s0_ref, s1_ref), (p0_ref, p1_ref), (a0_ref, a1_ref)
    lane = _iota((tq, LANES), 1)
    lo = lane < HEAD_DIM
    q0 = q0_ref[...] * FOX_SCALE
    q1 = q1_ref[...] * FOX_SCALE
    zero = jnp.zeros_like(q0)
    for r, blkq in enumerate((jnp.where(lo, q0, zero), jnp.where(lo, q1, zero),
                              jnp.where(lo, zero, q0), jnp.where(lo, zero, q1))):
        qbd_ref[r * tq:(r + 1) * tq, :] = blkq
        fqr_ref[r * tq:(r + 1) * tq, :] = jnp.broadcast_to(fq_ref[0, :, r:r + 1], (tq, LANES))
    _init_stats(m_ref, l_ref, acc_ref)

    def qk(j, buf):
        kb = k_ref[pl.ds(pl.multiple_of(j * tk, tk), tk), :].astype(BF16)
        s_refs[buf][...] = _nt(qbd_ref[...], kb)

    def pv(j, buf):
        vb = v_ref[pl.ds(pl.multiple_of(j * tk, tk), tk), :].astype(BF16)
        acc_ref[...] = acc_ref[...] * a_refs[buf][...] + _dot(p_refs[buf][...], vb)

    def softmax(j, buf, masked):
        for r in range(4):
            fk = [fk_ref[0, 0, r:r + 1, pl.ds(pl.multiple_of(j * tk + LANES * c, LANES), LANES)] for c in range(nl)]
            for t in range(tq // rt):
                r0 = r * tq + t * rt
                rows = slice(r0, r0 + rt)
                fq = fqr_ref[rows, :]
                tiles = []
                for c in range(nl):
                    s = s_refs[buf][rows, LANES * c:LANES * (c + 1)] + (fq - fk[c])
                    if masked:
                        s = jnp.where(LANES * c + _iota((1, LANES), 1) <= t * rt + _iota((rt, 1), 0), s, NEG)
                    tiles.append(s)
                _softmax_tile(tiles, rows, m_ref, l_ref, p_refs[buf], acc_ref, a_refs[buf])

    _causal_blocks(i, qk, softmax, pv)
    o = acc_ref[...] / l_ref[...]
    o_ref[0] = jnp.where(lo, o[0:tq], o[2 * tq:3 * tq]).astype(o_ref.dtype)
    o_ref[1] = jnp.where(lo, o[tq:2 * tq], o[3 * tq:4 * tq]).astype(o_ref.dtype)


def _fox_prompt(q, k, v, fq, fk, nb, seq, layer):
    tq, rt = 256, 32
    nq = seq // tq
    npair = FOX_KV_HEADS // 2
    rows = 4 * tq
    return pl.pallas_call(
        functools.partial(_fox_prompt_kernel, tq=tq, rt=rt),
        grid=(nb, npair, nq),
        in_specs=[pl.BlockSpec((tq, LANES), lambda b, p, i: (b * nq + i, p)),
                  pl.BlockSpec((tq, LANES), lambda b, p, i: (b * nq + i, npair + p)),
                  pl.BlockSpec((None, seq, LANES), lambda b, p, i: (layer, b, p)),
                  pl.BlockSpec((None, seq, LANES), lambda b, p, i: (layer, b, p)),
                  pl.BlockSpec((1, tq, 4), lambda b, p, i: (p, b * nq + i, 0)),
                  pl.BlockSpec((1, 1, 4, seq), lambda b, p, i: (b, p, 0, 0))],
        out_specs=pl.BlockSpec((2, tq, LANES), lambda b, p, i: (0, b * nq + i, p)),
        out_shape=jax.ShapeDtypeStruct((2, nb * seq, 512), BF16),
        scratch_shapes=[pltpu.VMEM((rows, LANES), BF16), pltpu.VMEM((rows, LANES), F32)]
        + _flash_scratch(rows, tq, LANES),
        compiler_params=_params("parallel", "parallel", "arbitrary"),
        name="fox_prompt",
    )(q, q, k, v, fq, fk)


def _page_specs(block, npg, layer, reverse_chunks=None):
    specs = []
    for pg in range(npg):
        if reverse_chunks is None:
            imap = lambda b, c, pt, pg=pg: (layer, pt[b, c * npg + pg], 0, 0)
        else:
            imap = lambda b, c, pt, pg=pg: (layer, pt[b, (reverse_chunks - 1 - c) * npg + pg], 0, 0)
        specs.append(pl.BlockSpec((1, 1) + block, imap))
    return specs


def _new_token_init(qbd16, kn, vn, bias_new, m_ref, l_ref, acc_ref, t_new):
    rows = qbd16.shape[0]
    s = _nt(qbd16, kn.astype(BF16))
    if bias_new is not None:
        s = s + bias_new
    tq = _iota((rows, t_new), 0) % t_new
    tk = _iota((rows, t_new), 1)
    s = jnp.where(tk <= tq, s, NEG)
    m = jnp.max(s, axis=-1, keepdims=True)
    p = jnp.exp(s - m)
    acc = jnp.zeros((rows, vn.shape[1]), F32)
    for t in range(t_new):
        acc = acc + p[:, t:t + 1] * vn[t:t + 1, :]
    m_ref[...] = jnp.broadcast_to(m, m_ref.shape)
    l_ref[...] = jnp.broadcast_to(jnp.sum(p, axis=-1, keepdims=True), l_ref.shape)
    acc_ref[...] = acc


def _mla_decode_kernel(pt_ref, q_ref, kn_ref, wuv_ref, *rest, npg, t_new, rt):
    ckv_refs = rest[:npg]
    kpe_refs = rest[npg:2 * npg]
    o_ref, qs_ref, ck_ref, kpt_ref, s_ref, p_ref, m_ref, l_ref, acc_ref = rest[2 * npg:]
    c = pl.program_id(1)
    nh = MLA_HEADS
    rows_all = nh * t_new
    tk = npg * PAGE

    @pl.when(c == 0)
    def _():
        q = q_ref[...]
        qs = jnp.concatenate([q[:, 256 * h:256 * h + 256] for h in range(nh)], axis=0).astype(BF16)
        qs_ref[...] = qs
        kn = kn_ref[...]
        _new_token_init(qs, kn, kn[:, 0:MLA_KV_RANK], None, m_ref, l_ref, acc_ref, t_new)

    for pg in range(npg):
        ck_ref[PAGE * pg:PAGE * (pg + 1), :] = ckv_refs[pg][0, 0].astype(BF16)
        kpt_ref[:, PAGE * pg:PAGE * (pg + 1)] = kpe_refs[pg][0, 0].astype(BF16)
    s_ref[...] = (_nt(qs_ref[:, 0:MLA_KV_RANK], ck_ref[...])
                  + _dot(qs_ref[:, MLA_KV_RANK:MLA_KV_RANK + MLA_ROPE], kpt_ref[...]))
    for t in range(rows_all // rt):
        rows = slice(t * rt, (t + 1) * rt)
        tiles = [s_ref[rows, LANES * cl:LANES * (cl + 1)] for cl in range(tk // LANES)]
        _softmax_tile(tiles, rows, m_ref, l_ref, p_ref, acc_ref)
    acc_ref[...] += _dot(p_ref[...], ck_ref[...])

    @pl.when(c == pl.num_programs(1) - 1)
    def _():
        o = acc_ref[...] / l_ref[...]
        for p in range(nh // 2):
            pair = jnp.concatenate([o[(2 * p) * t_new:(2 * p + 1) * t_new],
                                    o[(2 * p + 1) * t_new:(2 * p + 2) * t_new]], axis=1)
            o_ref[:, 128 * p:128 * p + 128] = _dot(pair.astype(BF16), wuv_ref[p]).astype(o_ref.dtype)


def _mla_decode(qcat, kcat, w_uv_bd, pool_ckv, pool_kpe_t, layer, page_table, t_new):
    nreq, n_pages = page_table.shape
    npg = min(MLA_PAGES_PER_STEP, n_pages)
    nchunk = n_pages // npg
    rows = MLA_HEADS * t_new
    tk = npg * PAGE
    grid_spec = pltpu.PrefetchScalarGridSpec(
        num_scalar_prefetch=1,
        grid=(nreq, nchunk),
        in_specs=[pl.BlockSpec((t_new, 2048), lambda b, c, pt: (b, 0)),
                  pl.BlockSpec((t_new, 256), lambda b, c, pt: (b, 0)),
                  pl.BlockSpec(w_uv_bd.shape, lambda b, c, pt: (0, 0, 0))]
        + _page_specs((PAGE, MLA_KV_RANK), npg, layer) + _page_specs((MLA_ROPE, PAGE), npg, layer),
        out_specs=pl.BlockSpec((t_new, 512), lambda b, c, pt: (b, 0)),
        scratch_shapes=[pltpu.VMEM((rows, 256), BF16), pltpu.VMEM((tk, MLA_KV_RANK), BF16),
                        pltpu.VMEM((MLA_ROPE, tk), BF16), pltpu.VMEM((rows, tk), F32), pltpu.VMEM((rows, tk), BF16),
                        pltpu.VMEM((rows, LANES), F32), pltpu.VMEM((rows, LANES), F32),
                        pltpu.VMEM((rows, MLA_KV_RANK), F32)],
    )
    return pl.pallas_call(
        functools.partial(_mla_decode_kernel, npg=npg, t_new=t_new, rt=16),
        grid_spec=grid_spec,
        out_shape=jax.ShapeDtypeStruct((nreq * t_new, 512), F32),
        compiler_params=_params("parallel", "arbitrary"),
        name="mla_decode",
    )(page_table, qcat, kcat, w_uv_bd, *([pool_ckv] * npg), *([pool_kpe_t] * npg))


def _moba_means_kernel(pt_ref, *rest, npg):
    k_refs = rest[:npg]
    o_ref = rest[npg]
    c = pl.program_id(1)
    ppb = MOBA_BLOCK // PAGE
    bpc = npg // ppb
    nblk = o_ref.shape[2]

    @pl.when(c == 0)
    def _():
        o_ref[...] = jnp.zeros(o_ref.shape, F32)

    lane = _iota((KV_WIDTH, nblk), 1)
    acc = o_ref[0]
    for j in range(bpc):
        tot = k_refs[ppb * j][0, 0]
        for pg in range(1, ppb):
            tot = tot + k_refs[ppb * j + pg][0, 0]
        col = jnp.sum(tot, axis=1, keepdims=True) * (1.0 / MOBA_BLOCK)
        acc = jnp.where(lane == c * bpc + j, col, acc)
    o_ref[0] = acc


def _moba_means(pool_k_t, layer, page_table):
    nreq, n_pages = page_table.shape
    npg = min(PAGES_PER_STEP, n_pages)
    nchunk = n_pages // npg
    nblk = n_pages * PAGE // MOBA_BLOCK
    grid_spec = pltpu.PrefetchScalarGridSpec(
        num_scalar_prefetch=1,
        grid=(nreq, nchunk),
        in_specs=_page_specs((KV_WIDTH, PAGE), npg, layer),
        out_specs=pl.BlockSpec((1, KV_WIDTH, nblk), lambda b, c, pt: (b, 0, 0)),
    )
    return pl.pallas_call(
        functools.partial(_moba_means_kernel, npg=npg),
        grid_spec=grid_spec,
        out_shape=jax.ShapeDtypeStruct((nreq, KV_WIDTH, nblk), F32),
        compiler_params=_params("parallel", "arbitrary"),
        name="moba_means",
    )(page_table, *([pool_k_t] * npg))


def _head_mask(rows_per_group, n_groups, head_of_group):
    rows = n_groups * rows_per_group
    grp = _iota((rows, KV_WIDTH), 0) // rows_per_group
    blk = _iota((rows, KV_WIDTH), 1) // HEAD_DIM
    return blk == head_of_group(grp)


DECODE_PARTS = 2


def _decode_scratch(rows, tk):
    tkp = tk // DECODE_PARTS
    per_part = lambda shape, dt: [pltpu.VMEM(shape, dt) for _ in range(DECODE_PARTS)]
    return (per_part((KV_WIDTH, tkp), BF16) + per_part((KV_WIDTH, tkp), BF16) + per_part((rows, tkp), F32)
            + per_part((rows, tkp), BF16) + per_part((rows, LANES), F32)
            + [pltpu.VMEM((rows, LANES), F32), pltpu.VMEM((rows, LANES), F32), pltpu.VMEM((rows, KV_WIDTH), F32)])


def _decode_chunk(k_refs, v_refs, qbd_ref, scratch, tile_bias, rows_all, rt):
    np_ = DECODE_PARTS
    kt_refs, vt_refs, s_refs, p_refs, a_refs = (scratch[np_ * n:np_ * (n + 1)] for n in range(5))
    m_ref, l_ref, acc_ref = scratch[5 * np_:]
    ppp = len(k_refs) // np_
    nl = ppp * PAGE // LANES
    for h in range(np_):
        for pg in range(ppp):
            kt_refs[h][:, PAGE * pg:PAGE * (pg + 1)] = k_refs[h * ppp + pg][0, 0].astype(BF16)
            vt_refs[h][:, PAGE * pg:PAGE * (pg + 1)] = v_refs[h * ppp + pg][0, 0].astype(BF16)
        s_refs[h][...] = _dot(qbd_ref[...], kt_refs[h][...])
    for h in range(np_):
        for t in range(rows_all // rt):
            rows = slice(t * rt, (t + 1) * rt)
            tiles = [s_refs[h][rows, LANES * cl:LANES * (cl + 1)] + tile_bias(t, h * nl + cl) for cl in range(nl)]
            _softmax_tile(tiles, rows, m_ref, l_ref, p_refs[h], acc_ref, a_refs[h])
        pv = _nt(p_refs[h][...], vt_refs[h][...])
        alpha = a_refs[h][...]
        for cl in range(KV_WIDTH // LANES):
            lanes = slice(LANES * cl, LANES * (cl + 1))
            acc_ref[:, lanes] = acc_ref[:, lanes] * alpha + pv[:, lanes]


def _fox_decode_kernel(pt_ref, q_ref, kn_ref, vn_ref, lfn_ref, *rest, npg, t_new, rt):
    k_refs = rest[:npg]
    v_refs = rest[npg:2 * npg]
    lf_refs = rest[2 * npg:3 * npg]
    o_ref, qbd_ref, sfx_ref, carry_ref, gq_ref = rest[3 * npg:3 * npg + 5]
    scratch = rest[3 * npg + 5:]
    m_ref, l_ref, acc_ref = scratch[-3:]
    c = pl.program_id(1)
    nhq = FOX_HEADS
    rows_all = nhq * t_new
    tk = npg * PAGE
    hmask = _head_mask(t_new, nhq, lambda grp: grp // 2)

    @pl.when(c == 0)
    def _():
        q = q_ref[...] * FOX_SCALE
        qrep = jnp.concatenate([q[:, KV_WIDTH * (hq % 2):KV_WIDTH * (hq % 2 + 1)] for hq in range(nhq)], axis=0)
        qbd16 = jnp.where(hmask, qrep, 0.0).astype(BF16)
        qbd_ref[...] = qbd16
        eye = (_iota((nhq, nhq), 0) == _iota((nhq, nhq), 1)).astype(F32)
        lfn = lfn_ref[...]
        tri = (_iota((t_new, t_new), 1) <= _iota((t_new, t_new), 0)).astype(F32)
        gct = _nt(eye, _dot(tri, lfn, precision=HI), precision=HI)
        g = jnp.concatenate([jnp.broadcast_to(gct[hq:hq + 1, :], (t_new, t_new)) for hq in range(nhq)], axis=0)
        tq = _iota((rows_all, t_new), 0) % t_new
        tl = _iota((rows_all, t_new), 1)
        gq = jnp.sum(jnp.where(tl == tq, g, 0.0), axis=-1, keepdims=True)
        gq_ref[...] = jnp.broadcast_to(gq, gq_ref.shape)
        carry_ref[...] = jnp.zeros(carry_ref.shape, F32)
        _new_token_init(qbd16, kn_ref[...], vn_ref[...], gq - g, m_ref, l_ref, acc_ref, t_new)

    lft = jnp.concatenate([r[0, 0] for r in lf_refs], axis=1)
    lane = _iota((nhq, tk), 1)
    incl = lft
    sh = 1
    while sh < tk:
        incl = incl + jnp.where(lane < tk - sh, pltpu.roll(incl, tk - sh, axis=1), 0.0)
        sh *= 2
    sfx_ref[...] = incl - lft + carry_ref[...]
    carry_ref[...] = carry_ref[...] + incl[:, 0:1]
    hq_per_tile = rt // t_new

    def tile_bias(t, cl):
        lanes = slice(LANES * cl, LANES * (cl + 1))
        sfx = jnp.concatenate([jnp.broadcast_to(sfx_ref[hq:hq + 1, lanes], (t_new, LANES))
                               for hq in range(t * hq_per_tile, (t + 1) * hq_per_tile)], axis=0)
        return sfx + gq_ref[t * rt:(t + 1) * rt, :]

    _decode_chunk(k_refs, v_refs, qbd_ref, scratch, tile_bias, rows_all, rt)

    @pl.when(c == pl.num_programs(1) - 1)
    def _():
        o = jnp.where(hmask, acc_ref[...] / l_ref[:, 0:1], 0.0)
        for g in range(2):
            tot = jnp.zeros((t_new, KV_WIDTH), F32)
            for kvh in range(FOX_KV_HEADS):
                r0 = (kvh * 2 + g) * t_new
                tot = tot + o[r0:r0 + t_new, :]
            o_ref[g] = tot


def _fox_decode(q, k_new, v_new, lf_new, pool_k_t, pool_v_t, pool_lf_t, layer, page_table, t_new):
    nreq, n_pages = page_table.shape
    npg = min(PAGES_PER_STEP, n_pages)
    nchunk = n_pages // npg
    rows = FOX_HEADS * t_new
    tk = npg * PAGE
    new = lambda w: pl.BlockSpec((t_new, w), lambda b, c, pt: (b, 0))
    grid_spec = pltpu.PrefetchScalarGridSpec(
        num_scalar_prefetch=1,
        grid=(nreq, nchunk),
        in_specs=[new(1024), new(KV_WIDTH), new(KV_WIDTH), new(FOX_HEADS)]
        + _page_specs((KV_WIDTH, PAGE), npg, layer, nchunk) + _page_specs((KV_WIDTH, PAGE), npg, layer, nchunk)
        + _page_specs((FOX_HEADS, PAGE), npg, layer, nchunk),
        out_specs=pl.BlockSpec((2, t_new, KV_WIDTH), lambda b, c, pt: (0, b, 0)),
        scratch_shapes=[pltpu.VMEM((rows, KV_WIDTH), BF16), pltpu.VMEM((FOX_HEADS, tk), F32),
                        pltpu.VMEM((FOX_HEADS, 1), F32), pltpu.VMEM((rows, LANES), F32)]
        + _decode_scratch(rows, tk),
    )
    return pl.pallas_call(
        functools.partial(_fox_decode_kernel, npg=npg, t_new=t_new, rt=16),
        grid_spec=grid_spec,
        out_shape=jax.ShapeDtypeStruct((2, nreq * t_new, KV_WIDTH), F32),
        compiler_params=_params("parallel", "arbitrary"),
        name="fox_decode",
    )(page_table, q, k_new, v_new, lf_new, *([pool_k_t] * npg), *([pool_v_t] * npg), *([pool_lf_t] * npg))


def _moba_decode_kernel(pt_ref, q_ref, kn_ref, vn_ref, mean_ref, slope_ref, *rest, npg, t_new, past, rt):
    k_refs = rest[:npg]
    v_refs = rest[npg:2 * npg]
    o_ref, qbd_ref, sel_ref, ab_ref, sl_ref = rest[2 * npg:2 * npg + 5]
    scratch = rest[2 * npg + 5:]
    m_ref, l_ref, acc_ref = scratch[-3:]
    c = pl.program_id(1)
    nh = MOBA_HEADS
    rows_all = nh * t_new
    tk = npg * PAGE
    bpc = tk // MOBA_BLOCK
    lpb = MOBA_BLOCK // LANES
    nblk = mean_ref.shape[2]
    hmask = _head_mask(t_new, nh, lambda grp: grp)

    @pl.when(c == 0)
    def _():
        q = q_ref[...]
        qbd = jnp.where(hmask, jnp.concatenate([q] * nh, axis=0), 0.0)
        qbd16 = (qbd * MOBA_SCALE).astype(BF16)
        qbd_ref[...] = qbd16
        sel_ref[...] = _top3_mask(_dot(qbd, mean_ref[0], precision=HI), nblk)
        slope = slope_ref[...]
        t_of_row = _iota((rows_all, 1), 0) % t_new
        ab_ref[...] = jnp.broadcast_to(-slope * (past + t_of_row).astype(F32), ab_ref.shape)
        sl_ref[...] = jnp.broadcast_to(slope, sl_ref.shape)
        bias_new = -slope * (t_of_row - _iota((1, t_new), 1)).astype(F32)
        _new_token_init(qbd16, kn_ref[...], vn_ref[...], bias_new, m_ref, l_ref, acc_ref, t_new)

    keep_bias = {}

    def tile_bias(t, cl):
        rows = slice(t * rt, (t + 1) * rt)
        j = cl // lpb
        if (t, j) not in keep_bias:
            chosen = jnp.sum(jnp.where(_iota((rt, nblk), 1) == c * bpc + j, sel_ref[rows, :], 0.0),
                             axis=-1, keepdims=True)
            keep_bias[(t, j)] = jnp.where(chosen > 0.0, 0.0, NEG)
        kposf = (c * tk + LANES * cl + _iota((1, LANES), 1)).astype(F32)
        return (ab_ref[rows, :] + sl_ref[rows, :] * kposf) + keep_bias[(t, j)]

    _decode_chunk(k_refs, v_refs, qbd_ref, scratch, tile_bias, rows_all, rt)

    @pl.when(c == pl.num_programs(1) - 1)
    def _():
        o = jnp.where(hmask, acc_ref[...] / l_ref[:, 0:1], 0.0)
        tot = jnp.zeros((t_new, KV_WIDTH), F32)
        for h in range(nh):
            tot = tot + o[h * t_new:(h + 1) * t_new, :]
        o_ref[...] = tot


def _moba_decode(q, k_new, v_new, means_t, slopes, pool_k_t, pool_v_t, layer, page_table, t_new):
    nreq, n_pages = page_table.shape
    npg = min(PAGES_PER_STEP, n_pages)
    nchunk = n_pages // npg
    rows = MOBA_HEADS * t_new
    tk = npg * PAGE
    nblk = means_t.shape[2]
    new = lambda w: pl.BlockSpec((t_new, w), lambda b, c, pt: (b, 0))
    grid_spec = pltpu.PrefetchScalarGridSpec(
        num_scalar_prefetch=1,
        grid=(nreq, nchunk),
        in_specs=[new(KV_WIDTH), new(KV_WIDTH), new(KV_WIDTH),
                  pl.BlockSpec((1, KV_WIDTH, nblk), lambda b, c, pt: (b, 0, 0)),
                  pl.BlockSpec((rows, 1), lambda b, c, pt: (0, 0))]
        + _page_specs((KV_WIDTH, PAGE), npg, layer) + _page_specs((KV_WIDTH, PAGE), npg, layer),
        out_specs=pl.BlockSpec((t_new, KV_WIDTH), lambda b, c, pt: (b, 0)),
        scratch_shapes=[pltpu.VMEM((rows, KV_WIDTH), BF16), pltpu.VMEM((rows, nblk), F32),
                        pltpu.VMEM((rows, LANES), F32), pltpu.VMEM((rows, LANES), F32)]
        + _decode_scratch(rows, tk),
    )
    return pl.pallas_call(
        functools.partial(_moba_decode_kernel, npg=npg, t_new=t_new, past=n_pages * PAGE, rt=16),
        grid_spec=grid_spec,
        out_shape=jax.ShapeDtypeStruct((nreq * t_new, KV_WIDTH), F32),
        compiler_params=_params("parallel", "arbitrary"),
        name="moba_decode",
    )(page_table, q, k_new, v_new, means_t, slopes, *([pool_k_t] * npg), *([pool_v_t] * npg))


def _rot_cols(w):
    half = MLA_ROPE // 2
    return jnp.concatenate([-w[..., half:], w[..., :half]], axis=-1)


def _pad_cols(w, width):
    return jnp.pad(w, [(0, 0)] * (w.ndim - 1) + [(0, width - w.shape[-1])])


def _prep_even(w_in, w_uq, w_uk, w_uv):
    o = MLA_Q_RANK + MLA_KV_RANK
    pe = w_in[:, o:o + MLA_ROPE]
    w_lat = w_in[:, :o].astype(BF16)
    w_pe = jnp.concatenate([_pad_cols(pe, 128), _pad_cols(_rot_cols(pe), 128)], axis=1).astype(BF16)
    w_moba = w_in[:, o + MLA_ROPE:].astype(BF16)
    nope = w_uq[:, :, :MLA_NOPE].reshape(MLA_Q_RANK, MLA_HEADS * MLA_NOPE)
    qpe = w_uq[:, :, MLA_NOPE:]
    w_uq_x = jnp.concatenate([nope, _pad_cols(qpe, 128).reshape(MLA_Q_RANK, -1),
                              _pad_cols(_rot_cols(qpe), 128).reshape(MLA_Q_RANK, -1)], axis=1).astype(BF16)
    uk = jnp.transpose(w_uk, (1, 2, 0))
    uv = jnp.transpose(w_uv, (1, 0, 2))
    z_k = jnp.zeros((MLA_NOPE, MLA_KV_RANK), F32)
    z_v = jnp.zeros((MLA_KV_RANK, MLA_V), F32)
    uk_bd = jnp.stack([jnp.block([[uk[2 * p], z_k], [z_k, uk[2 * p + 1]]]) for p in range(MLA_HEADS // 2)])
    uv_bd = jnp.stack([jnp.block([[uv[2 * p], z_v], [z_v, uv[2 * p + 1]]]) for p in range(MLA_HEADS // 2)])
    return (w_lat, w_pe, w_moba), w_uq_x, uk_bd.astype(BF16), uv_bd.astype(BF16)


def _prep_odd(w_in, w_out):
    nq = FOX_HEADS * HEAD_DIM
    nk = FOX_KV_HEADS * HEAD_DIM
    wq = w_in[:, :nq].reshape(D_MODEL, FOX_KV_HEADS, 2, HEAD_DIM).transpose(0, 2, 1, 3).reshape(D_MODEL, nq)
    w_kv = w_in[:, nq:nq + 2 * nk]
    w_f = _pad_cols(w_in[:, nq + 2 * nk:], 128)
    w_out_x = w_out.reshape(FOX_KV_HEADS, 2, HEAD_DIM, D_MODEL).transpose(1, 0, 2, 3).reshape(nq, D_MODEL)
    return (wq.astype(BF16), w_kv.astype(BF16), w_f.astype(BF16)), w_out_x.astype(BF16)


def _rope_tables(pos):
    half = MLA_ROPE // 2
    inv = ROPE_THETA ** (-jnp.arange(half, dtype=F32) / half)
    ang = pos.astype(F32)[:, None] * inv[None, :]
    cos, sin = jnp.cos(ang), jnp.sin(ang)
    return (_pad_cols(jnp.concatenate([cos, cos], axis=-1), 128),
            _pad_cols(jnp.concatenate([sin, sin], axis=-1), 128))


def _row_tile(m):
    for t in (512, 256, 128, 64, 32, 16, 8):
        if m % t == 0:
            return t
    raise ValueError(f"row count {m} is not a multiple of 8")


def _kv_pool_t(pool):
    l, n, pg, h, d = pool.shape
    return jnp.transpose(pool, (0, 1, 3, 4, 2)).reshape(l, n, h * d, pg)


def kernel(x_prompt, x_sample, cache_mla_ckv, cache_mla_kpe, cache_moba_k, cache_moba_v, cache_fox_k, cache_fox_v, cache_fox_logf, page_table, attn_norm, ffn_norm, final_norm, w_in_even, mla_q_norm, mla_kv_norm, mla_w_uq, mla_w_uk, mla_w_uv, w_out_even, w_in_odd, fox_b_forget, w_out_odd, ffn_w_gate, ffn_w_up, ffn_w_down):
    nb, seq, d = x_prompt.shape
    nreq, t_new, _ = x_sample.shape
    n_pages = page_table.shape[1]
    past = n_pages * PAGE
    depth = attn_norm.shape[0]
    assert d == D_MODEL and seq % MOBA_BLOCK == 0 and past % MOBA_BLOCK == 0 and t_new == 8
    assert n_pages % min(PAGES_PER_STEP, n_pages) == 0 and n_pages % min(MLA_PAGES_PER_STEP, n_pages) == 0
    mp, ms = nb * seq, nreq * t_new
    tmp, tms = _row_tile(mp), _row_tile(ms)

    xp = x_prompt.reshape(mp, d)
    xs = x_sample.reshape(ms, d)
    cs_p, sn_p = (jnp.tile(t, (nb, 1)) for t in _rope_tables(jnp.arange(seq, dtype=jnp.int32)))
    cs_s, sn_s = (jnp.tile(t, (nreq, 1)) for t in _rope_tables(past + jnp.arange(t_new, dtype=jnp.int32)))
    slope_h = jnp.exp2(-8.0 * jnp.arange(1, MOBA_HEADS + 1, dtype=F32) / MOBA_HEADS)
    slopes_p = jnp.repeat(slope_h, MOBA_BLOCK).reshape(MOBA_HEADS // 2, 2 * MOBA_BLOCK, 1)
    slopes_s = jnp.repeat(slope_h, t_new).reshape(MOBA_HEADS * t_new, 1)
    pool_moba_k = _kv_pool_t(cache_moba_k)
    pool_moba_v = _kv_pool_t(cache_moba_v)
    pool_fox_k = _kv_pool_t(cache_fox_k)
    pool_fox_v = _kv_pool_t(cache_fox_v)
    pool_fox_lf = jnp.transpose(cache_fox_logf, (0, 1, 3, 2))
    pool_mla_kpe = jnp.transpose(cache_mla_kpe, (0, 1, 3, 2))
    row1 = lambda v: v.reshape(1, -1)

    outs = {k: [] for k in ("ckv_p", "ckv_s", "kpe_p", "kpe_s", "fl_p", "fl_s")}
    kv = {"mp": None, "ms": None, "fp": None, "fs": None}
    n_even, n_odd = (depth + 1) // 2, depth // 2
    assert MOBA_HEADS == FOX_KV_HEADS and n_odd > 0
    for l in range(depth):
        final = l == depth - 1
        ffn = (row1(ffn_norm[l]), ffn_w_gate[l].astype(BF16), ffn_w_up[l].astype(BF16),
               ffn_w_down[l].astype(BF16), row1(final_norm))
        if l % 2 == 0:
            e = l // 2
            w_in_x, w_uq_x, uk_bd, uv_bd = _prep_even(w_in_even[e], mla_w_uq[e], mla_w_uk[e], mla_w_uv[e])
            w_out = w_out_even[e].astype(BF16)
            common = (row1(attn_norm[l]), *w_in_x, row1(mla_q_norm[e]), row1(mla_kv_norm[e]), w_uq_x, uk_bd)
            qcat, kcat, ckv, kpe, mq, mk, mv = _proj_even(xp, *common, cs_p, sn_p, tmp, BF16, e, n_even, kv["mp"])
            kv["mp"] = (mk, mv)
            mix_a = _mla_prompt(qcat, kcat, uv_bd, nb, seq)
            mix_b = _moba_prompt(mq, mk, mv, slopes_p, nb, seq, e)
            xp = _out_ffn(xp, mix_a, mix_b, w_out, *ffn, tmp, final)
            outs["ckv_p"].append(ckv.reshape(nb, seq, MLA_KV_RANK))
            outs["kpe_p"].append(kpe.reshape(nb, seq, MLA_ROPE))

            qcat, kcat, ckv, kpe, mq, mk, mv = _proj_even(xs, *common, cs_s, sn_s, tms, F32, e, n_even, kv["ms"])
            kv["ms"] = (mk, mv)
            mix_a = _mla_decode(qcat, kcat, uv_bd, cache_mla_ckv, pool_mla_kpe, e, page_table, t_new)
            means_t = _moba_means(pool_moba_k, e, page_table)
            mix_b = _moba_decode(mq, mk[e], mv[e], means_t, slopes_s, pool_moba_k, pool_moba_v, e, page_table, t_new)
            xs = _out_ffn(xs, mix_a, mix_b, w_out, *ffn, tms, final)
            outs["ckv_s"].append(ckv.reshape(nreq, t_new, MLA_KV_RANK))
            outs["kpe_s"].append(kpe.reshape(nreq, t_new, MLA_ROPE))
        else:
            o = l // 2
            w_in_x, w_out = _prep_odd(w_in_odd[o], w_out_odd[o])
            common = (row1(attn_norm[l]), *w_in_x, row1(fox_b_forget[o]))
            q, k, v, lf = _proj_odd(xp, *common, tmp, BF16, o, n_odd, kv["fp"])
            kv["fp"] = (k, v)
            f, ft = _fox_cumsum(lf, lf.reshape(nb, seq, FOX_HEADS).transpose(0, 2, 1), nb, seq)
            fq = f.reshape(mp, 4, 4).transpose(1, 0, 2)
            mix = _fox_prompt(q, k, v, fq, ft.reshape(nb, 4, 4, seq), nb, seq, o)
            xp = _out_ffn(xp, mix[0], mix[1], w_out, *ffn, tmp, final)
            outs["fl_p"].append(lf.reshape(nb, seq, FOX_HEADS))

            q, k, v, lf = _proj_odd(xs, *common, tms, F32, o, n_odd, kv["fs"])
            kv["fs"] = (k, v)
            mix = _fox_decode(q, k[o], v[o], lf, pool_fox_k, pool_fox_v, pool_fox_lf, o, page_table, t_new)
            xs = _out_ffn(xs, mix[0], mix[1], w_out, *ffn, tms, final)
            outs["fl_s"].append(lf.reshape(nreq, t_new, FOX_HEADS))

    st = lambda name: jnp.stack(outs[name])
    heads_p = lambda a: a.reshape(a.shape[0], nb, seq, MOBA_HEADS, HEAD_DIM)
    heads_s = lambda a: a.reshape(a.shape[0], nreq, t_new, MOBA_HEADS, HEAD_DIM)
    return (xp.reshape(nb, seq, d), xs.reshape(nreq, t_new, d),
            st("ckv_p"), st("ckv_s"), st("kpe_p"), st("kpe_s"),
            heads_p(kv["mp"][0]), heads_s(kv["ms"][0]), heads_p(kv["mp"][1]), heads_s(kv["ms"][1]),
            heads_p(kv["fp"][0]), heads_s(kv["fs"][0]), heads_p(kv["fp"][1]), heads_s(kv["fs"][1]),
            st("fl_p"), st("fl_s"))
```
